```python
import jax, jax.numpy as jnp
from jax import lax
import numpy as np

D_MODEL = 2048
BATCH = 2
SEQ = 16384
DEPTH = 1

N_META = 16
GRID_W = 64
HEAD_DIM = 128
N_Q_HEADS = 8
N_KV_HEADS = 2
Q_PER_KV = N_Q_HEADS // N_KV_HEADS
ATTN_WIDTH = N_Q_HEADS * HEAD_DIM
KV_WIDTH = N_KV_HEADS * HEAD_DIM
MIX_WIDTH = D_MODEL
POOL_WIDTH = MIX_WIDTH - ATTN_WIDTH
POOL_WINDOWS = (2, 4, 8, 16)
N_POOL_GROUPS = len(POOL_WINDOWS)
POOL_GROUP = POOL_WIDTH // N_POOL_GROUPS
IN_WIDTH = ATTN_WIDTH + 2 * KV_WIDTH + POOL_WIDTH
Q_BLOCK = 128
ROPE_THETA = 10000.0
ROPE_AXIS_DIM = HEAD_DIM // 2
N_EXPERT_GROUPS = 4
EXPERTS_PER_GROUP = 8
N_EXPERTS = N_EXPERT_GROUPS * EXPERTS_PER_GROUP
TOP_K = 2
D_EXPERT = D_MODEL // 4
EXPERT_BLOCK = 128
EPS = 1e-6

kernel_name = "hymba_pool_axialgqa_hiermoe_encoder"


def _rms_norm(x, g):
    xf = x.astype(jnp.float32)
    y = xf * lax.rsqrt(jnp.mean(xf * xf, axis=-1, keepdims=True) + EPS)
    return (y * g.astype(jnp.float32)).astype(x.dtype)


def _axial_angles(n_tok):
    rows = n_tok // GRID_W
    t = jnp.arange(rows * GRID_W)
    zeros = jnp.zeros((N_META,), jnp.float32)
    row = jnp.concatenate([zeros, (t // GRID_W).astype(jnp.float32)])
    col = jnp.concatenate([zeros, (t % GRID_W).astype(jnp.float32)])
    inv_freq = jnp.power(ROPE_THETA, -jnp.arange(0, ROPE_AXIS_DIM, 2, dtype=jnp.float32) / ROPE_AXIS_DIM)
    ang_r = row[:, None] * inv_freq[None, :]
    ang_c = col[:, None] * inv_freq[None, :]
    return jnp.cos(ang_r), jnp.sin(ang_r), jnp.cos(ang_c), jnp.sin(ang_c)


def _rotate(x, cos, sin):
    half = x.shape[-1] // 2
    x1, x2 = x[..., :half], x[..., half:]
    c = cos[None, :, None, :]
    s = sin[None, :, None, :]
    return jnp.concatenate([x1 * c - x2 * s, x2 * c + x1 * s], axis=-1)


def _axial_rope(x, rope):
    cr, sr, cc, sc = rope
    xf = x.astype(jnp.float32)
    out = jnp.concatenate([_rotate(xf[..., :ROPE_AXIS_DIM], cr, sr),
                           _rotate(xf[..., ROPE_AXIS_DIM:], cc, sc)], axis=-1)
    return out.astype(x.dtype)


def _axial_gqa(q, k, v, q_g, k_g, rope):
    B, L, _ = q.shape
    q = _axial_rope(_rms_norm(q.reshape(B, L, N_Q_HEADS, HEAD_DIM), q_g), rope)
    k = _axial_rope(_rms_norm(k.reshape(B, L, N_KV_HEADS, HEAD_DIM), k_g), rope)
    v = v.reshape(B, L, N_KV_HEADS, HEAD_DIM)
    n_blk = -(-L // Q_BLOCK)
    Lp = n_blk * Q_BLOCK
    q = jnp.pad(q, ((0, 0), (0, Lp - L), (0, 0), (0, 0)))
    qb = q.reshape(B, n_blk, Q_BLOCK, N_KV_HEADS, Q_PER_KV, HEAD_DIM).transpose(1, 0, 2, 3, 4, 5)
    scale = HEAD_DIM ** -0.5

    def block(q_blk):
        s = jnp.einsum('bqkgd,bskd->bkgqs', q_blk, k).astype(jnp.float32) * scale
        p = jax.nn.softmax(s, axis=-1).astype(v.dtype)
        return jnp.einsum('bkgqs,bskd->bqkgd', p, v)

    o = lax.map(block, qb)
    o = o.transpose(1, 0, 2, 3, 4, 5).reshape(B, Lp, ATTN_WIDTH)
    return o[:, :L]


def _multiscale_pool(p, w, scale):
    B, L, C = p.shape
    pf = p.astype(jnp.float32)
    cs = jnp.pad(jnp.cumsum(pf, axis=1), ((0, 0), (1, 0), (0, 0)))
    t = jnp.arange(L)
    outs = []
    for gi, win in enumerate(POOL_WINDOWS):
        lo_c, hi_c = gi * POOL_GROUP, (gi + 1) * POOL_GROUP
        lo = jnp.clip(t - win // 2, 0, L)
        hi = jnp.clip(t - win // 2 + win, 0, L)
        csg = cs[:, :, lo_c:hi_c]
        mean = (csg[:, hi] - csg[:, lo]) / (hi - lo).astype(jnp.float32)[None, :, None]
        outs.append(mean - pf[:, :, lo_c:hi_c])
    m = jnp.stack(outs, axis=2).astype(p.dtype)
    y = jnp.einsum('blgc,gcd->blgd', m, w).reshape(B, L, C)
    return y * scale


def _hier_moe(xs, w_rg, b_rg, w_re, b_re, w_gate, w_up, w_down):
    N, D = xs.shape
    g_prob = jax.nn.softmax((xs @ w_rg).astype(jnp.float32) + b_rg.astype(jnp.float32), axis=-1)
    g_p, g_idx = lax.top_k(g_prob, 1)
    e_logits = ((xs @ w_re).astype(jnp.float32) + b_re.astype(jnp.float32)).reshape(N, N_EXPERT_GROUPS, EXPERTS_PER_GROUP)
    sel = jnp.broadcast_to(g_idx[:, :, None], (N, 1, EXPERTS_PER_GROUP))
    e_prob = jax.nn.softmax(jnp.take_along_axis(e_logits, sel, axis=1)[:, 0], axis=-1)
    e_p, e_idx = lax.top_k(e_prob, TOP_K)
    gates = g_p * (e_p / jnp.sum(e_p, axis=-1, keepdims=True))
    expert = g_idx * EXPERTS_PER_GROUP + e_idx

    M = N * TOP_K
    flat_e = expert.reshape(-1)
    flat_g = gates.reshape(-1)
    flat_t = jnp.repeat(jnp.arange(N, dtype=jnp.int32), TOP_K)
    order = jnp.argsort(flat_e)
    s_e, s_t, s_g = flat_e[order], flat_t[order], flat_g[order]
    counts = jnp.bincount(flat_e, length=N_EXPERTS)
    starts = jnp.cumsum(counts) - counts
    padded = (counts + EXPERT_BLOCK - 1) // EXPERT_BLOCK * EXPERT_BLOCK
    pstarts = jnp.cumsum(padded) - padded
    pends = pstarts + padded
    dest = pstarts[s_e] + (jnp.arange(M) - starts[s_e])
    P = (-(-M // EXPERT_BLOCK) + N_EXPERTS) * EXPERT_BLOCK
    n_blk = P // EXPERT_BLOCK
    slot_t = jnp.zeros((P,), jnp.int32).at[dest].set(s_t)
    slot_g = jnp.zeros((P,), jnp.float32).at[dest].set(s_g)
    blk_start = jnp.arange(n_blk) * EXPERT_BLOCK
    blk_e = jnp.minimum(jnp.sum(pends[None, :] <= blk_start[:, None], axis=1), N_EXPERTS - 1)

    def expert_block(args):
        tok, g, e = args
        xb = xs[tok]
        h = jax.nn.silu(xb @ w_gate[e]) * (xb @ w_up[e])
        return (h @ w_down[e]) * g[:, None].astype(xs.dtype)

    out = lax.map(expert_block, (slot_t.reshape(n_blk, EXPERT_BLOCK),
                                 slot_g.reshape(n_blk, EXPERT_BLOCK), blk_e))
    return jnp.zeros_like(xs).at[slot_t].add(out.reshape(P, D))


def setup_inputs(seed: int = 0) -> dict:
    key = jax.random.key(seed)
    ks = jax.random.split(key, 20)
    f = jnp.float32
    nrm = lambda k, shape, s: jax.random.normal(k, shape, f) * s
    return {
        "x": nrm(ks[0], (BATCH, SEQ, D_MODEL), 1.0),
        "meta_tokens": nrm(ks[1], (N_META, D_MODEL), 1.0),
        "norm1_g": 1.0 + nrm(ks[2], (DEPTH, D_MODEL), 0.02),
        "w_in": nrm(ks[3], (DEPTH, D_MODEL, IN_WIDTH), D_MODEL ** -0.5),
        "q_norm_g": 1.0 + nrm(ks[4], (DEPTH, HEAD_DIM), 0.02),
        "k_norm_g": 1.0 + nrm(ks[5], (DEPTH, HEAD_DIM), 0.02),
        "pool_w": nrm(ks[6], (DEPTH, N_POOL_GROUPS, POOL_GROUP, POOL_GROUP), POOL_GROUP ** -0.5),
        "pool_scale": 1.0 + nrm(ks[7], (DEPTH, POOL_WIDTH), 0.02),
        "w_out": nrm(ks[8], (DEPTH, MIX_WIDTH, D_MODEL), MIX_WIDTH ** -0.5),
        "norm2_g": 1.0 + nrm(ks[9], (DEPTH, D_MODEL), 0.02),
        "w_router_group": nrm(ks[10], (DEPTH, D_MODEL, N_EXPERT_GROUPS), D_MODEL ** -0.5),
        "b_router_group": nrm(ks[11], (DEPTH, N_EXPERT_GROUPS), 0.01),
        "w_router_expert": nrm(ks[12], (DEPTH, D_MODEL, N_EXPERTS), D_MODEL ** -0.5),
        "b_router_expert": nrm(ks[13], (DEPTH, N_EXPERTS), 0.01),
        "w_gate": nrm(ks[14], (DEPTH, N_EXPERTS, D_MODEL, D_EXPERT), D_MODEL ** -0.5),
        "w_up": nrm(ks[15], (DEPTH, N_EXPERTS, D_MODEL, D_EXPERT), D_MODEL ** -0.5),
        "w_down": nrm(ks[16], (DEPTH, N_EXPERTS, D_EXPERT, D_MODEL), D_EXPERT ** -0.5),
    }


def reference(x, meta_tokens, norm1_g, w_in, q_norm_g, k_norm_g, pool_w, pool_scale, w_out,
              norm2_g, w_router_group, b_router_group, w_router_expert, b_router_expert,
              w_gate, w_up, w_down):
    B, S, D = x.shape
    meta = jnp.broadcast_to(meta_tokens[None].astype(x.dtype), (B, N_META, D))
    h = jnp.concatenate([meta, x], axis=1)
    L = h.shape[1]
    rope = _axial_angles(S)
    q_end = ATTN_WIDTH
    k_end = q_end + KV_WIDTH
    v_end = k_end + KV_WIDTH
    for l in range(DEPTH):
        a = _rms_norm(h, norm1_g[l])
        proj = a @ w_in[l]
        attn = _axial_gqa(proj[..., :q_end], proj[..., q_end:k_end], proj[..., k_end:v_end],
                          q_norm_g[l], k_norm_g[l], rope)
        pool = _multiscale_pool(proj[..., v_end:], pool_w[l], pool_scale[l])
        h = h + jnp.concatenate([attn, pool], axis=-1) @ w_out[l]
        b = _rms_norm(h, norm2_g[l]).reshape(B * L, D)
        h = h + _hier_moe(b, w_router_group[l], b_router_group[l], w_router_expert[l],
                          b_router_expert[l], w_gate[l], w_up[l], w_down[l]).reshape(B, L, D)
    return h[:, N_META:]
```

```python
import functools
import math

import jax
import jax.numpy as jnp
from jax import lax
from jax.experimental import pallas as pl
from jax.experimental.pallas import tpu as pltpu

N_META = 16
GRID_W = 64
HEAD_DIM = 128
N_Q_HEADS = 8
N_KV_HEADS = 2
Q_PER_KV = N_Q_HEADS // N_KV_HEADS
ATTN_WIDTH = N_Q_HEADS * HEAD_DIM
KV_WIDTH = N_KV_HEADS * HEAD_DIM
POOL_WINDOWS = (2, 4, 8, 16)
N_POOL_GROUPS = len(POOL_WINDOWS)
POOL_GROUP = 256
POOL_WIDTH = N_POOL_GROUPS * POOL_GROUP
ROPE_THETA = 10000.0
ROPE_AXIS_DIM = HEAD_DIM // 2
N_EXPERT_GROUPS = 4
EXPERTS_PER_GROUP = 8
N_EXPERTS = N_EXPERT_GROUPS * EXPERTS_PER_GROUP
TOP_K = 2
EPS = 1e-6

HALO = 8
ROUTER_LANES = 128
GROUP_LANE0 = N_EXPERTS
LOG2E = math.log2(math.e)
Q_SCALE = HEAD_DIM ** -0.5 * LOG2E

V7X_VMEM_LIMIT = 56 * 1024 * 1024

F32 = jnp.float32
BF16 = jnp.bfloat16
_NT = (((1,), (1,)), ((), ()))


def _tile(n, pref):
    t = min(n, pref)
    while n % t:
        t //= 2
    return t


def _inproj_kernel(x_ref, g1_ref, w_ref, cos_ref, sin_ref, qg_ref, kg_ref,
                   q_ref, k_ref, v_ref, p_ref):
    tm = x_ref.shape[0]
    xf = x_ref[...]
    ms = jnp.mean(xf * xf, axis=-1, keepdims=True)
    a = (xf * lax.rsqrt(ms + EPS) * g1_ref[...]).astype(BF16)
    cos = cos_ref[...]
    sin = sin_ref[...]
    lane = lax.broadcasted_iota(jnp.int32, (tm, HEAD_DIM), 1)
    first_half = (lane % ROPE_AXIS_DIM) < (ROPE_AXIS_DIM // 2)

    def norm_rope(hd, g, scale):
        n = hd * lax.rsqrt(jnp.mean(hd * hd, axis=-1, keepdims=True) + EPS) * g
        partner = jnp.where(first_half,
                            pltpu.roll(n, HEAD_DIM - ROPE_AXIS_DIM // 2, 1),
                            pltpu.roll(n, ROPE_AXIS_DIM // 2, 1))
        return ((n * cos + partner * sin) * scale).astype(BF16)

    qg = qg_ref[...]
    kg = kg_ref[...]
    for c in range(N_Q_HEADS // 2):
        pr = jnp.dot(a, w_ref[:, c * 256:(c + 1) * 256], preferred_element_type=F32)
        for j in range(2):
            q_ref[0, 2 * c + j] = norm_rope(pr[:, j * HEAD_DIM:(j + 1) * HEAD_DIM], qg, Q_SCALE)
    pr = jnp.dot(a, w_ref[:, ATTN_WIDTH:ATTN_WIDTH + KV_WIDTH], preferred_element_type=F32)
    for j in range(N_KV_HEADS):
        k_ref[0, j] = norm_rope(pr[:, j * HEAD_DIM:(j + 1) * HEAD_DIM], kg, 1.0)
    pr = jnp.dot(a, w_ref[:, ATTN_WIDTH + KV_WIDTH:ATTN_WIDTH + 2 * KV_WIDTH],
                 preferred_element_type=F32)
    for j in range(N_KV_HEADS):
        v_ref[0, j] = pr[:, j * HEAD_DIM:(j + 1) * HEAD_DIM].astype(BF16)
    p_ref[...] = jnp.dot(a, w_ref[:, ATTN_WIDTH + 2 * KV_WIDTH:], preferred_element_type=F32)


def _inproj(x2d, g1, w_bf, cos, sin, qg, kg, batch, seq, tm):
    n, d = x2d.shape
    nj = seq // tm
    in_w = w_bf.shape[1]
    const = lambda i: (0, 0)
    return pl.pallas_call(
        _inproj_kernel,
        grid=(n // tm,),
        in_specs=[
            pl.BlockSpec((tm, d), lambda i: (i, 0)),
            pl.BlockSpec((1, d), const),
            pl.BlockSpec((d, in_w), const),
            pl.BlockSpec((tm, HEAD_DIM), lambda i: (i % nj, 0)),
            pl.BlockSpec((tm, HEAD_DIM), lambda i: (i % nj, 0)),
            pl.BlockSpec((1, HEAD_DIM), const),
            pl.BlockSpec((1, HEAD_DIM), const),
        ],
        out_specs=[
            pl.BlockSpec((1, N_Q_HEADS, tm, HEAD_DIM), lambda i: (i // nj, 0, i % nj, 0)),
            pl.BlockSpec((1, N_KV_HEADS, tm, HEAD_DIM), lambda i: (i // nj, 0, i % nj, 0)),
            pl.BlockSpec((1, N_KV_HEADS, tm, HEAD_DIM), lambda i: (i // nj, 0, i % nj, 0)),
            pl.BlockSpec((tm, POOL_WIDTH), lambda i: (i, 0)),
        ],
        out_shape=[
            jax.ShapeDtypeStruct((batch, N_Q_HEADS, seq, HEAD_DIM), BF16),
            jax.ShapeDtypeStruct((batch, N_KV_HEADS, seq, HEAD_DIM), BF16),
            jax.ShapeDtypeStruct((batch, N_KV_HEADS, seq, HEAD_DIM), BF16),
            jax.ShapeDtypeStruct((n, POOL_WIDTH), F32),
        ],
        compiler_params=pltpu.CompilerParams(
            dimension_semantics=("arbitrary",), vmem_limit_bytes=V7X_VMEM_LIMIT),
        name="inproj",
    )(x2d, g1, w_bf, cos, sin, qg, kg)


def _attn_kernel(q_ref, k_ref, v_ref, km_ref, vm_ref, o_ref, *, tk):
    tq = q_ref.shape[2]
    seq = k_ref.shape[2]
    q = q_ref[0].reshape(Q_PER_KV * tq, HEAD_DIM)

    s0 = lax.dot_general(q, km_ref[0], _NT, preferred_element_type=F32)
    m0 = jnp.max(s0, axis=1, keepdims=True)
    p0 = jnp.exp2(s0 - m0)
    l0 = jnp.sum(p0, axis=1, keepdims=True)
    acc0 = jnp.dot(p0.astype(BF16), vm_ref[0], preferred_element_type=F32)

    def body(j, carry):
        m, l, acc = carry
        off = pl.multiple_of(j * tk, tk)
        kj = k_ref[0, 0, pl.ds(off, tk), :]
        vj = v_ref[0, 0, pl.ds(off, tk), :]
        s = lax.dot_general(q, kj, _NT, preferred_element_type=F32)
        m_new = jnp.maximum(m, jnp.max(s, axis=1, keepdims=True))
        alpha = jnp.exp2(m - m_new)
        p = jnp.exp2(s - m_new)
        l = alpha * l + jnp.sum(p, axis=1, keepdims=True)
        acc = alpha * acc + jnp.dot(p.astype(BF16), vj, preferred_element_type=F32)
        return m_new, l, acc

    _, l, acc = lax.fori_loop(0, seq // tk, body, (m0, l0, acc0))
    out = acc / l
    for g in range(Q_PER_KV):
        o_ref[0, :, g * HEAD_DIM:(g + 1) * HEAD_DIM] = out[g * tq:(g + 1) * tq].astype(BF16)


def _attention(q, k, v, km, vm, tq, tk):
    batch, _, seq, _ = q.shape
    gw = Q_PER_KV * HEAD_DIM
    return pl.pallas_call(
        functools.partial(_attn_kernel, tk=tk),
        grid=(batch, N_KV_HEADS, seq // tq),
        in_specs=[
            pl.BlockSpec((1, Q_PER_KV, tq, HEAD_DIM), lambda b, h, i: (b, h, i, 0)),
            pl.BlockSpec((1, 1, seq, HEAD_DIM), lambda b, h, i: (b, h, 0, 0)),
            pl.BlockSpec((1, 1, seq, HEAD_DIM), lambda b, h, i: (b, h, 0, 0)),
            pl.BlockSpec((1, N_META, HEAD_DIM), lambda b, h, i: (h, 0, 0)),
            pl.BlockSpec((1, N_META, HEAD_DIM), lambda b, h, i: (h, 0, 0)),
        ],
        out_specs=pl.BlockSpec((1, tq, gw), lambda b, h, i: (b, i, h)),
        out_shape=jax.ShapeDtypeStruct((batch, seq, ATTN_WIDTH), BF16),
        compiler_params=pltpu.CompilerParams(
            dimension_semantics=("arbitrary", "arbitrary", "arbitrary"),
            vmem_limit_bytes=V7X_VMEM_LIMIT),
        name="attention",
    )(q, k, v, km, vm)


def _post_kernel(attn_ref, pp_ref, prev_ref, next_ref, pm_ref, x_ref, pw_ref, ps_ref, wo_ref,
                 g2_ref, wr_ref, br_ref, h1_ref, b_ref, ri_ref, rg_ref, ext_ref, *, nj, seq):
    tm = x_ref.shape[0]
    j = pl.program_id(0) % nj
    ext_ref[0:HALO] = jnp.where(j == 0, pm_ref[N_META - HALO:N_META, :], prev_ref[...])
    ext_ref[HALO:HALO + tm] = pp_ref[...]
    ext_ref[HALO + tm:2 * HALO + tm] = jnp.where(j == nj - 1, 0.0, next_ref[...])

    t = j * tm + lax.broadcasted_iota(jnp.int32, (tm, 1), 0)
    acc = jnp.dot(attn_ref[...], wo_ref[0:ATTN_WIDTH, :], preferred_element_type=F32)
    for g, win in enumerate(POOL_WINDOWS):
        c0, c1 = g * POOL_GROUP, (g + 1) * POOL_GROUP
        half = win // 2
        tot = ext_ref[HALO - half:HALO - half + tm, c0:c1]
        for d in range(-half + 1, half):
            tot = tot + ext_ref[HALO + d:HALO + d + tm, c0:c1]
        cnt = (jnp.minimum(t + half, seq) - (t - half)).astype(F32)
        mg = (tot / cnt - pp_ref[:, c0:c1]).astype(BF16)
        yg = jnp.dot(mg, pw_ref[g], preferred_element_type=F32) * ps_ref[:, c0:c1]
        acc = acc + jnp.dot(yg.astype(BF16), wo_ref[ATTN_WIDTH + c0:ATTN_WIDTH + c1, :],
                            preferred_element_type=F32)

    h1 = x_ref[...] + acc
    h1_ref[...] = h1
    ms = jnp.mean(h1 * h1, axis=-1, keepdims=True)
    bf = h1 * lax.rsqrt(ms + EPS) * g2_ref[...]
    b_ref[...] = bf
    logits = jnp.dot(bf.astype(BF16), wr_ref[...], preferred_element_type=F32) + br_ref[...]

    lane = lax.broadcasted_iota(jnp.int32, (tm, ROUTER_LANES), 1)
    lane_f = lane.astype(F32)
    neg = jnp.float32(-jnp.inf)
    big = jnp.float32(ROUTER_LANES)
    is_group = (lane >= GROUP_LANE0) & (lane < GROUP_LANE0 + N_EXPERT_GROUPS)
    gl = jnp.where(is_group, logits, neg)
    gmax = jnp.max(gl, axis=1, keepdims=True)
    gidx = jnp.min(jnp.where(gl == gmax, lane_f, big), axis=1, keepdims=True) - GROUP_LANE0
    g_p = 1.0 / jnp.sum(jnp.exp(gl - gmax), axis=1, keepdims=True)
    in_group = (lane // EXPERTS_PER_GROUP).astype(F32) == gidx
    el = jnp.where(in_group & (lane < N_EXPERTS), logits, neg)
    m1 = jnp.max(el, axis=1, keepdims=True)
    i1 = jnp.min(jnp.where(el == m1, lane_f, big), axis=1, keepdims=True)
    el2 = jnp.where(lane_f == i1, neg, el)
    m2 = jnp.max(el2, axis=1, keepdims=True)
    i2 = jnp.min(jnp.where(el2 == m2, lane_f, big), axis=1, keepdims=True)
    p2 = jnp.exp(m2 - m1)
    den = 1.0 + p2
    gate1 = g_p / den
    gate2 = g_p * p2 / den
    ri_ref[...] = jnp.where(lane == 0, i1, jnp.where(lane == 1, i2, 0.0)).astype(jnp.int32)
    rg_ref[...] = jnp.where(lane == 0, gate1, jnp.where(lane == 1, gate2, 0.0))


def _post(attn2d, pp, pm, x2d, pw_bf, ps, wo_bf, g2, wr_bf, br, seq, tm):
    n, d = x2d.shape
    nj = seq // tm
    hb = tm // HALO
    last_hb = n // HALO - 1
    const2 = lambda i: (0, 0)
    row = lambda i: (i, 0)
    return pl.pallas_call(
        functools.partial(_post_kernel, nj=nj, seq=seq),
        grid=(n // tm,),
        in_specs=[
            pl.BlockSpec((tm, ATTN_WIDTH), row),
            pl.BlockSpec((tm, POOL_WIDTH), row),
            pl.BlockSpec((HALO, POOL_WIDTH), lambda i: (jnp.maximum(i * hb - 1, 0), 0)),
            pl.BlockSpec((HALO, POOL_WIDTH), lambda i: (jnp.minimum((i + 1) * hb, last_hb), 0)),
            pl.BlockSpec((N_META, POOL_WIDTH), const2),
            pl.BlockSpec((tm, d), row),
            pl.BlockSpec((N_POOL_GROUPS, POOL_GROUP, POOL_GROUP), lambda i: (0, 0, 0)),
            pl.BlockSpec((1, POOL_WIDTH), const2),
            pl.BlockSpec((ATTN_WIDTH + POOL_WIDTH, d), const2),
            pl.BlockSpec((1, d), const2),
            pl.BlockSpec((d, ROUTER_LANES), const2),
            pl.BlockSpec((1, ROUTER_LANES), const2),
        ],
        out_specs=[
            pl.BlockSpec((tm, d), row),
            pl.BlockSpec((tm, d), row),
            pl.BlockSpec((tm, ROUTER_LANES), row),
            pl.BlockSpec((tm, ROUTER_LANES), row),
        ],
        out_shape=[
            jax.ShapeDtypeStruct((n, d), F32),
            jax.ShapeDtypeStruct((n, d), F32),
            jax.ShapeDtypeStruct((n, ROUTER_LANES), jnp.int32),
            jax.ShapeDtypeStruct((n, ROUTER_LANES), F32),
        ],
        scratch_shapes=[pltpu.VMEM((tm + 2 * HALO, POOL_WIDTH), F32)],
        compiler_params=pltpu.CompilerParams(
            dimension_semantics=("arbitrary",), vmem_limit_bytes=V7X_VMEM_LIMIT),
        name="post",
    )(attn2d, pp, pp, pp, pm, x2d, pw_bf, ps, wo_bf, g2, wr_bf, br)


def _expert_kernel(blk_e_ref, nvalid_ref, slot_t_ref, b_hbm, wg_ref, wu_ref, wd_ref,
                   y_ref, xbuf, sem, *, bm):
    i = pl.program_id(0)
    nvalid = nvalid_ref[0]
    slot = i % 2

    def row_copy(blk, r, s):
        t = slot_t_ref[blk * bm + r]
        return pltpu.make_async_copy(b_hbm.at[pl.ds(t, 1)], xbuf.at[s, pl.ds(r, 1)], sem.at[s])

    def issue(blk, s):
        def body(r, c):
            row_copy(blk, r, s).start()
            return c
        lax.fori_loop(0, bm, body, 0)

    @pl.when(jnp.logical_and(i == 0, nvalid > 0))
    def _():
        issue(0, 0)

    @pl.when(i + 1 < nvalid)
    def _():
        issue(i + 1, 1 - slot)

    @pl.when(i < nvalid)
    def _():
        def body(r, c):
            row_copy(i, r, slot).wait()
            return c
        lax.fori_loop(0, bm, body, 0)
        xb = xbuf[slot].astype(BF16)
        hg = jnp.dot(xb, wg_ref[0], preferred_element_type=F32)
        hu = jnp.dot(xb, wu_ref[0], preferred_element_type=F32)
        h = (hg * jax.nn.sigmoid(hg) * hu).astype(BF16)
        y_ref[...] = jnp.dot(h, wd_ref[0], preferred_element_type=F32)

    @pl.when(i >= nvalid)
    def _():
        y_ref[...] = jnp.zeros_like(y_ref)


def _experts(blk_e, nvalid, slot_t, b2d, wg_bf, wu_bf, wd_bf, bm):
    n, d = b2d.shape
    nblk = blk_e.shape[0]
    de = wg_bf.shape[2]
    grid_spec = pltpu.PrefetchScalarGridSpec(
        num_scalar_prefetch=3,
        grid=(nblk,),
        in_specs=[
            pl.BlockSpec(memory_space=pl.ANY),
            pl.BlockSpec((1, d, de), lambda i, be, nv, st: (be[i], 0, 0)),
            pl.BlockSpec((1, d, de), lambda i, be, nv, st: (be[i], 0, 0)),
            pl.BlockSpec((1, de, d), lambda i, be, nv, st: (be[i], 0, 0)),
        ],
        out_specs=pl.BlockSpec((bm, d), lambda i, be, nv, st: (i, 0)),
        scratch_shapes=[pltpu.VMEM((2, bm, d), F32), pltpu.SemaphoreType.DMA((2,))],
    )
    return pl.pallas_call(
        functools.partial(_expert_kernel, bm=bm),
        grid_spec=grid_spec,
        out_shape=jax.ShapeDtypeStruct((nblk * bm, d), F32),
        compiler_params=pltpu.CompilerParams(
            dimension_semantics=("arbitrary",), vmem_limit_bytes=V7X_VMEM_LIMIT),
        name="experts",
    )(blk_e, nvalid, slot_t, b2d, wg_bf, wu_bf, wd_bf)


def _combine_kernel(dest_ref, h1_ref, rg_ref, y_hbm, out_ref, ybuf, sem):
    tm = h1_ref.shape[0]
    i = pl.program_id(0)
    n_steps = pl.num_programs(0)
    slot = i % 2

    def row_copy(tile, r, k, s):
        p = dest_ref[(tile * tm + r) * TOP_K + k]
        return pltpu.make_async_copy(y_hbm.at[pl.ds(p, 1)], ybuf.at[s, k, pl.ds(r, 1)], sem.at[s])

    def issue(tile, s):
        def body(r, c):
            for k in range(TOP_K):
                row_copy(tile, r, k, s).start()
            return c
        lax.fori_loop(0, tm, body, 0)

    @pl.when(i == 0)
    def _():
        issue(0, 0)

    @pl.when(i + 1 < n_steps)
    def _():
        issue(i + 1, 1 - slot)

    def wait_body(r, c):
        for k in range(TOP_K):
            row_copy(i, r, k, slot).wait()
        return c
    lax.fori_loop(0, tm, wait_body, 0)

    rg = rg_ref[...]
    out = h1_ref[...]
    for k in range(TOP_K):
        out = out + rg[:, k:k + 1] * ybuf[slot, k]
    out_ref[...] = out


def _combine(dest, h1, rg, y, tm):
    n, d = h1.shape
    row = lambda i, de: (i, 0)
    grid_spec = pltpu.PrefetchScalarGridSpec(
        num_scalar_prefetch=1,
        grid=(n // tm,),
        in_specs=[
            pl.BlockSpec((tm, d), row),
            pl.BlockSpec((tm, ROUTER_LANES), row),
            pl.BlockSpec(memory_space=pl.ANY),
        ],
        out_specs=pl.BlockSpec((tm, d), row),
        scratch_shapes=[pltpu.VMEM((2, TOP_K, tm, d), F32), pltpu.SemaphoreType.DMA((2,))],
    )
    return pl.pallas_call(
        _combine_kernel,
        grid_spec=grid_spec,
        out_shape=jax.ShapeDtypeStruct((n, d), F32),
        compiler_params=pltpu.CompilerParams(
            dimension_semantics=("arbitrary",), vmem_limit_bytes=V7X_VMEM_LIMIT),
        name="combine",
    )(dest, h1, rg, y)


def _rope_tables(seq):
    t = jnp.arange(seq)
    inv_freq = jnp.power(ROPE_THETA, -jnp.arange(0, ROPE_AXIS_DIM, 2, dtype=F32) / ROPE_AXIS_DIM)
    ang_r = (t // GRID_W).astype(F32)[:, None] * inv_freq[None, :]
    ang_c = (t % GRID_W).astype(F32)[:, None] * inv_freq[None, :]
    cr, sr, cc, sc = jnp.cos(ang_r), jnp.sin(ang_r), jnp.cos(ang_c), jnp.sin(ang_c)
    cos = jnp.concatenate([cr, cr, cc, cc], axis=-1)
    sin = jnp.concatenate([-sr, sr, -sc, sc], axis=-1)
    return cos, sin


def _routing_plan(expert, bm):
    n = expert.shape[0]
    m = n * TOP_K
    flat_e = expert.reshape(-1)
    order = jnp.argsort(flat_e).astype(jnp.int32)
    s_e = flat_e[order]
    counts = jnp.sum(flat_e[:, None] == jnp.arange(N_EXPERTS, dtype=jnp.int32)[None, :], axis=0,
                     dtype=jnp.int32)
    starts = jnp.cumsum(counts) - counts
    padded = (counts + bm - 1) // bm * bm
    pends = jnp.cumsum(padded)
    pstarts = pends - padded
    dest_sorted = (pstarts[s_e] + (jnp.arange(m, dtype=jnp.int32) - starts[s_e])).astype(jnp.int32)
    nblk = -(-m // bm) + N_EXPERTS
    slot_t = jnp.zeros((nblk * bm,), jnp.int32).at[dest_sorted].set(order // TOP_K)
    dest = jnp.zeros((m,), jnp.int32).at[order].set(dest_sorted)
    blk_start = jnp.arange(nblk, dtype=jnp.int32) * bm
    blk_e = jnp.minimum(jnp.sum(pends[None, :] <= blk_start[:, None], axis=1), N_EXPERTS - 1)
    nvalid = (pends[-1] // bm).reshape(1)
    return blk_e.astype(jnp.int32), nvalid.astype(jnp.int32), slot_t, dest


def kernel(x, meta_tokens, norm1_g, w_in, q_norm_g, k_norm_g, pool_w, pool_scale, w_out, norm2_g,
           w_router_group, b_router_group, w_router_expert, b_router_expert, w_gate, w_up, w_down):
    batch, seq, d = x.shape
    n = batch * seq
    assert norm1_g.shape[0] == 1, "single-layer block"
    assert seq % GRID_W == 0 and seq % 256 == 0

    tm_in = _tile(seq, 512)
    tm_post = _tile(seq, 256)
    tq = _tile(seq, 128)
    tk = _tile(seq, 512)
    bm = 256
    tm_comb = _tile(seq, 256)

    x2d = x.reshape(n, d)
    w_in_bf = w_in[0].astype(BF16)
    cos, sin = _rope_tables(seq)
    qg = q_norm_g[0].reshape(1, HEAD_DIM)
    kg = k_norm_g[0].reshape(1, HEAD_DIM)
    g1 = norm1_g[0].reshape(1, d)

    q, k, v, pp = _inproj(x2d, g1, w_in_bf, cos, sin, qg, kg, batch, seq, tm_in)
    ones = jnp.ones((N_META, HEAD_DIM), F32)
    _, km, vm, pm = _inproj(meta_tokens.astype(F32), g1, w_in_bf, ones, jnp.zeros_like(ones),
                            qg, kg, 1, N_META, N_META)

    attn = _attention(q, k, v, km[0], vm[0], tq, tk)

    wr = jnp.concatenate([w_router_expert[0], w_router_group[0],
                          jnp.zeros((d, ROUTER_LANES - N_EXPERTS - N_EXPERT_GROUPS), F32)], axis=1)
    br = jnp.concatenate([b_router_expert[0], b_router_group[0],
                          jnp.zeros((ROUTER_LANES - N_EXPERTS - N_EXPERT_GROUPS,), F32)])
    h1, b2d, ri, rg = _post(attn.reshape(n, ATTN_WIDTH), pp, pm, x2d, pool_w[0].astype(BF16),
                            pool_scale[0].reshape(1, POOL_WIDTH), w_out[0].astype(BF16),
                            norm2_g[0].reshape(1, d), wr.astype(BF16),
                            br.reshape(1, ROUTER_LANES), seq, tm_post)

    blk_e, nvalid, slot_t, dest = _routing_plan(ri[:, :TOP_K], bm)
    y = _experts(blk_e, nvalid, slot_t, b2d, w_gate[0].astype(BF16), w_up[0].astype(BF16),
                 w_down[0].astype(BF16), bm)
    out = _combine(dest, h1, rg, y, tm_comb)
    return out.reshape(batch, seq, d)
```

```python
import functools
import math

import jax
import jax.numpy as jnp
from jax import lax
from jax.experimental import pallas as pl
from jax.experimental.pallas import tpu as pltpu

N_META = 16
GRID_W = 64
HEAD_DIM = 128
N_Q_HEADS = 8
N_KV_HEADS = 2
Q_PER_KV = N_Q_HEADS // N_KV_HEADS
ATTN_WIDTH = N_Q_HEADS * HEAD_DIM
KV_WIDTH = N_KV_HEADS * HEAD_DIM
POOL_WINDOWS = (2, 4, 8, 16)
N_POOL_GROUPS = len(POOL_WINDOWS)
POOL_GROUP = 256
POOL_WIDTH = N_POOL_GROUPS * POOL_GROUP
ROPE_THETA = 10000.0
ROPE_AXIS_DIM = HEAD_DIM // 2
N_EXPERT_GROUPS = 4
EXPERTS_PER_GROUP = 8
N_EXPERTS = N_EXPERT_GROUPS * EXPERTS_PER_GROUP
TOP_K = 2
EPS = 1e-6

V_ONES_ROWS = 16
V_ROWS = HEAD_DIM + V_ONES_ROWS
HALO = 8
ROUTER_LANES = 128
GROUP_LANE0 = N_EXPERTS
LOG2E = math.log2(math.e)
Q_SCALE = HEAD_DIM ** -0.5 * LOG2E

V7X_VMEM_LIMIT = 56 * 1024 * 1024

F32 = jnp.float32
BF16 = jnp.bfloat16
_NT = (((1,), (1,)), ((), ()))


def _tile(n, pref):
    t = min(n, pref)
    while n % t:
        t //= 2
    return t


def _inproj_kernel(x_ref, g1_ref, w_ref, cos_ref, sin_ref, qg_ref, kg_ref,
                   qt_ref, k_ref, vt_ref, p_ref):
    tm = x_ref.shape[0]
    xf = x_ref[...]
    ms = jnp.mean(xf * xf, axis=-1, keepdims=True)
    a = (xf * lax.rsqrt(ms + EPS) * g1_ref[...]).astype(BF16)
    cos = cos_ref[...]
    sin = sin_ref[...]
    lane = lax.broadcasted_iota(jnp.int32, (tm, HEAD_DIM), 1)
    first_half = (lane % ROPE_AXIS_DIM) < (ROPE_AXIS_DIM // 2)

    def norm_rope(hd, g, scale):
        n = hd * lax.rsqrt(jnp.mean(hd * hd, axis=-1, keepdims=True) + EPS) * g
        partner = jnp.where(first_half,
                            pltpu.roll(n, HEAD_DIM - ROPE_AXIS_DIM // 2, 1),
                            pltpu.roll(n, ROPE_AXIS_DIM // 2, 1))
        return (n * cos + partner * sin) * scale

    qg = qg_ref[...]
    kg = kg_ref[...]
    for c in range(N_Q_HEADS // 2):
        pr = jnp.dot(a, w_ref[:, c * 256:(c + 1) * 256], preferred_element_type=F32)
        for j in range(2):
            qh = norm_rope(pr[:, j * HEAD_DIM:(j + 1) * HEAD_DIM], qg, Q_SCALE)
            qt_ref[0, 2 * c + j] = qh.T.astype(BF16)
    pr = jnp.dot(a, w_ref[:, ATTN_WIDTH:ATTN_WIDTH + KV_WIDTH], preferred_element_type=F32)
    for j in range(N_KV_HEADS):
        k_ref[0, j] = norm_rope(pr[:, j * HEAD_DIM:(j + 1) * HEAD_DIM], kg, 1.0).astype(BF16)
    pr = jnp.dot(a, w_ref[:, ATTN_WIDTH + KV_WIDTH:ATTN_WIDTH + 2 * KV_WIDTH],
                 preferred_element_type=F32)
    for j in range(N_KV_HEADS):
        vt_ref[0, j, 0, 0:HEAD_DIM, :] = pr[:, j * HEAD_DIM:(j + 1) * HEAD_DIM].T.astype(BF16)
        vt_ref[0, j, 0, HEAD_DIM:V_ROWS, :] = jnp.ones((V_ONES_ROWS, tm), BF16)
    p_ref[...] = jnp.dot(a, w_ref[:, ATTN_WIDTH + 2 * KV_WIDTH:], preferred_element_type=F32)


def _inproj(x2d, g1, w_bf, cos, sin, qg, kg, batch, seq, tm):
    n, d = x2d.shape
    nj = seq // tm
    in_w = w_bf.shape[1]
    const = lambda i: (0, 0)
    return pl.pallas_call(
        _inproj_kernel,
        grid=(n // tm,),
        in_specs=[
            pl.BlockSpec((tm, d), lambda i: (i, 0)),
            pl.BlockSpec((1, d), const),
            pl.BlockSpec((d, in_w), const),
            pl.BlockSpec((tm, HEAD_DIM), lambda i: (i % nj, 0)),
            pl.BlockSpec((tm, HEAD_DIM), lambda i: (i % nj, 0)),
            pl.BlockSpec((1, HEAD_DIM), const),
            pl.BlockSpec((1, HEAD_DIM), const),
        ],
        out_specs=[
            pl.BlockSpec((1, N_Q_HEADS, HEAD_DIM, tm), lambda i: (i // nj, 0, 0, i % nj)),
            pl.BlockSpec((1, N_KV_HEADS, tm, HEAD_DIM), lambda i: (i // nj, 0, i % nj, 0)),
            pl.BlockSpec((1, N_KV_HEADS, 1, V_ROWS, tm), lambda i: (i // nj, 0, i % nj, 0, 0)),
            pl.BlockSpec((tm, POOL_WIDTH), lambda i: (i, 0)),
        ],
        out_shape=[
            jax.ShapeDtypeStruct((batch, N_Q_HEADS, HEAD_DIM, seq), BF16),
            jax.ShapeDtypeStruct((batch, N_KV_HEADS, seq, HEAD_DIM), BF16),
            jax.ShapeDtypeStruct((batch, N_KV_HEADS, nj, V_ROWS, tm), BF16),
            jax.ShapeDtypeStruct((n, POOL_WIDTH), F32),
        ],
        compiler_params=pltpu.CompilerParams(
            dimension_semantics=("arbitrary",), vmem_limit_bytes=V7X_VMEM_LIMIT),
        name="inproj",
    )(x2d, g1, w_bf, cos, sin, qg, kg)


def _attn_kernel(qt_ref, k_ref, vt_ref, km_ref, vmt_ref, o_ref, s_scr, acc_scr):
    tq = qt_ref.shape[3]
    nk = vt_ref.shape[2]
    tk = vt_ref.shape[4]
    unroll = 4 if nk % 4 == 0 else 2
    qt = jnp.concatenate([qt_ref[0, g] for g in range(Q_PER_KV)], axis=1)

    def scores(j):
        off = pl.multiple_of(j * tk, tk)
        return jnp.dot(k_ref[0, 0, pl.ds(off, tk), :], qt, preferred_element_type=F32)

    s0 = jnp.dot(km_ref[0], qt, preferred_element_type=F32)
    m0 = jnp.max(s0, axis=0, keepdims=True)
    p0 = jnp.exp2(s0 - m0).astype(BF16)
    acc_scr[...] = jnp.dot(vmt_ref[0], p0, preferred_element_type=F32)
    s = scores(0)
    s_scr[0] = s
    mt0 = jnp.max(s, axis=0, keepdims=True)

    def step(j, cur, m, mt):
        s_next = scores(jnp.minimum(j + 1, nk - 1))
        s_scr[1 - cur] = s_next
        mt_next = jnp.max(s_next, axis=0, keepdims=True)
        m_new = jnp.maximum(m, mt)
        alpha = jnp.exp2(m - m_new)
        p = jnp.exp2(s_scr[cur] - m_new).astype(BF16)
        pv = jnp.dot(vt_ref[0, 0, j], p, preferred_element_type=F32)
        acc_scr[...] = alpha * acc_scr[...] + pv
        return m_new, mt_next

    def body(jb, carry):
        m, mt = carry
        for u in range(unroll):
            m, mt = step(unroll * jb + u, u % 2, m, mt)
        return m, mt

    lax.fori_loop(0, nk // unroll, body, (m0, mt0))
    out_t = acc_scr[0:HEAD_DIM, :] / acc_scr[HEAD_DIM:HEAD_DIM + 1, :]
    for g in range(Q_PER_KV):
        o_ref[0, :, g * HEAD_DIM:(g + 1) * HEAD_DIM] = out_t[:, g * tq:(g + 1) * tq].T.astype(BF16)


def _attention(qt, k, vt, km, vmt, tq):
    batch, _, _, seq = qt.shape
    nk, tk = vt.shape[2], vt.shape[4]
    assert nk % 2 == 0
    gw = Q_PER_KV * HEAD_DIM
    return pl.pallas_call(
        _attn_kernel,
        grid=(batch, N_KV_HEADS, seq // tq),
        in_specs=[
            pl.BlockSpec((1, Q_PER_KV, HEAD_DIM, tq), lambda b, h, i: (b, h, 0, i)),
            pl.BlockSpec((1, 1, seq, HEAD_DIM), lambda b, h, i: (b, h, 0, 0)),
            pl.BlockSpec((1, 1, nk, V_ROWS, tk), lambda b, h, i: (b, h, 0, 0, 0)),
            pl.BlockSpec((1, N_META, HEAD_DIM), lambda b, h, i: (h, 0, 0)),
            pl.BlockSpec((1, V_ROWS, N_META), lambda b, h, i: (h, 0, 0)),
        ],
        out_specs=pl.BlockSpec((1, tq, gw), lambda b, h, i: (b, i, h)),
        out_shape=jax.ShapeDtypeStruct((batch, seq, ATTN_WIDTH), BF16),
        scratch_shapes=[pltpu.VMEM((2, tk, Q_PER_KV * tq), F32),
                        pltpu.VMEM((V_ROWS, Q_PER_KV * tq), F32)],
        compiler_params=pltpu.CompilerParams(
            dimension_semantics=("arbitrary", "arbitrary", "arbitrary"),
            vmem_limit_bytes=V7X_VMEM_LIMIT),
        name="attention",
    )(qt, k, vt, km, vmt)


def _post_kernel(attn_ref, pp_ref, prev_ref, next_ref, pm_ref, x_ref, pw_ref, ps_ref, wo_ref,
                 g2_ref, wr_ref, br_ref, h1_ref, b_ref, ri_ref, rg_ref, ext_ref, *, nj, seq):
    tm = x_ref.shape[0]
    j = pl.program_id(0) % nj
    ext_ref[0:HALO] = jnp.where(j == 0, pm_ref[N_META - HALO:N_META, :], prev_ref[...])
    ext_ref[HALO:HALO + tm] = pp_ref[...]
    ext_ref[HALO + tm:2 * HALO + tm] = jnp.where(j == nj - 1, 0.0, next_ref[...])

    t = j * tm + lax.broadcasted_iota(jnp.int32, (tm, 1), 0)
    acc = jnp.dot(attn_ref[...], wo_ref[0:ATTN_WIDTH, :], preferred_element_type=F32)
    for g, win in enumerate(POOL_WINDOWS):
        c0, c1 = g * POOL_GROUP, (g + 1) * POOL_GROUP
        half = win // 2
        tot = ext_ref[HALO - half:HALO - half + tm, c0:c1]
        for d in range(-half + 1, half):
            tot = tot + ext_ref[HALO + d:HALO + d + tm, c0:c1]
        cnt = (jnp.minimum(t + half, seq) - (t - half)).astype(F32)
        mg = (tot / cnt - pp_ref[:, c0:c1]).astype(BF16)
        yg = jnp.dot(mg, pw_ref[g], preferred_element_type=F32) * ps_ref[:, c0:c1]
        acc = acc + jnp.dot(yg.astype(BF16), wo_ref[ATTN_WIDTH + c0:ATTN_WIDTH + c1, :],
                            preferred_element_type=F32)

    h1 = x_ref[...] + acc
    h1_ref[...] = h1
    ms = jnp.mean(h1 * h1, axis=-1, keepdims=True)
    bf = h1 * lax.rsqrt(ms + EPS) * g2_ref[...]
    b_ref[...] = bf
    logits = jnp.dot(bf.astype(BF16), wr_ref[...], preferred_element_type=F32) + br_ref[...]

    lane = lax.broadcasted_iota(jnp.int32, (tm, ROUTER_LANES), 1)
    lane_f = lane.astype(F32)
    neg = jnp.float32(-jnp.inf)
    big = jnp.float32(ROUTER_LANES)
    is_group = (lane >= GROUP_LANE0) & (lane < GROUP_LANE0 + N_EXPERT_GROUPS)
    gl = jnp.where(is_group, logits, neg)
    gmax = jnp.max(gl, axis=1, keepdims=True)
    gidx = jnp.min(jnp.where(gl == gmax, lane_f, big), axis=1, keepdims=True) - GROUP_LANE0
    g_p = 1.0 / jnp.sum(jnp.exp(gl - gmax), axis=1, keepdims=True)
    in_group = (lane // EXPERTS_PER_GROUP).astype(F32) == gidx
    el = jnp.where(in_group & (lane < N_EXPERTS), logits, neg)
    m1 = jnp.max(el, axis=1, keepdims=True)
    i1 = jnp.min(jnp.where(el == m1, lane_f, big), axis=1, keepdims=True)
    el2 = jnp.where(lane_f == i1, neg, el)
    m2 = jnp.max(el2, axis=1, keepdims=True)
    i2 = jnp.min(jnp.where(el2 == m2, lane_f, big), axis=1, keepdims=True)
    p2 = jnp.exp(m2 - m1)
    den = 1.0 + p2
    gate1 = g_p / den
    gate2 = g_p * p2 / den
    ri_ref[...] = jnp.where(lane == 0, i1, jnp.where(lane == 1, i2, 0.0)).astype(jnp.int32)
    rg_ref[...] = jnp.where(lane == 0, gate1, jnp.where(lane == 1, gate2, 0.0))


def _post(attn2d, pp, pm, x2d, pw_bf, ps, wo_bf, g2, wr_bf, br, seq, tm):
    n, d = x2d.shape
    nj = seq // tm
    hb = tm // HALO
    last_hb = n // HALO - 1
    const2 = lambda i: (0, 0)
    row = lambda i: (i, 0)
    return pl.pallas_call(
        functools.partial(_post_kernel, nj=nj, seq=seq),
        grid=(n // tm,),
        in_specs=[
            pl.BlockSpec((tm, ATTN_WIDTH), row),
            pl.BlockSpec((tm, POOL_WIDTH), row),
            pl.BlockSpec((HALO, POOL_WIDTH), lambda i: (jnp.maximum(i * hb - 1, 0), 0)),
            pl.BlockSpec((HALO, POOL_WIDTH), lambda i: (jnp.minimum((i + 1) * hb, last_hb), 0)),
            pl.BlockSpec((N_META, POOL_WIDTH), const2),
            pl.BlockSpec((tm, d), row),
            pl.BlockSpec((N_POOL_GROUPS, POOL_GROUP, POOL_GROUP), lambda i: (0, 0, 0)),
            pl.BlockSpec((1, POOL_WIDTH), const2),
            pl.BlockSpec((ATTN_WIDTH + POOL_WIDTH, d), const2),
            pl.BlockSpec((1, d), const2),
            pl.BlockSpec((d, ROUTER_LANES), const2),
            pl.BlockSpec((1, ROUTER_LANES), const2),
        ],
        out_specs=[
            pl.BlockSpec((tm, d), row),
            pl.BlockSpec((tm, d), row),
            pl.BlockSpec((tm, ROUTER_LANES), row),
            pl.BlockSpec((tm, ROUTER_LANES), row),
        ],
        out_shape=[
            jax.ShapeDtypeStruct((n, d), F32),
            jax.ShapeDtypeStruct((n, d), F32),
            jax.ShapeDtypeStruct((n, ROUTER_LANES), jnp.int32),
            jax.ShapeDtypeStruct((n, ROUTER_LANES), F32),
        ],
        scratch_shapes=[pltpu.VMEM((tm + 2 * HALO, POOL_WIDTH), F32)],
        compiler_params=pltpu.CompilerParams(
            dimension_semantics=("arbitrary",), vmem_limit_bytes=V7X_VMEM_LIMIT),
        name="post",
    )(attn2d, pp, pp, pp, pm, x2d, pw_bf, ps, wo_bf, g2, wr_bf, br)


def _expert_kernel(blk_e_ref, nvalid_ref, slot_t_ref, b_hbm, wg_ref, wu_ref, wd_ref,
                   y_ref, xbuf, sem, *, bm):
    i = pl.program_id(0)
    nvalid = nvalid_ref[0]
    slot = i % 2

    def row_copy(blk, r, s):
        t = slot_t_ref[blk * bm + r]
        return pltpu.make_async_copy(b_hbm.at[pl.ds(t, 1)], xbuf.at[s, pl.ds(r, 1)], sem.at[s])

    def issue(blk, s):
        def body(r, c):
            row_copy(blk, r, s).start()
            return c
        lax.fori_loop(0, bm, body, 0)

    @pl.when(jnp.logical_and(i == 0, nvalid > 0))
    def _():
        issue(0, 0)

    @pl.when(i + 1 < nvalid)
    def _():
        issue(i + 1, 1 - slot)

    @pl.when(i < nvalid)
    def _():
        def body(r, c):
            row_copy(i, r, slot).wait()
            return c
        lax.fori_loop(0, bm, body, 0)
        xb = xbuf[slot].astype(BF16)
        hg = jnp.dot(xb, wg_ref[0], preferred_element_type=F32)
        hu = jnp.dot(xb, wu_ref[0], preferred_element_type=F32)
        h = (hg * jax.nn.sigmoid(hg) * hu).astype(BF16)
        y_ref[...] = jnp.dot(h, wd_ref[0], preferred_element_type=F32)

    @pl.when(i >= nvalid)
    def _():
        y_ref[...] = jnp.zeros_like(y_ref)


def _experts(blk_e, nvalid, slot_t, b2d, wg_bf, wu_bf, wd_bf, bm):
    n, d = b2d.shape
    nblk = blk_e.shape[0]
    de = wg_bf.shape[2]
    grid_spec = pltpu.PrefetchScalarGridSpec(
        num_scalar_prefetch=3,
        grid=(nblk,),
        in_specs=[
            pl.BlockSpec(memory_space=pl.ANY),
            pl.BlockSpec((1, d, de), lambda i, be, nv, st: (be[i], 0, 0)),
            pl.BlockSpec((1, d, de), lambda i, be, nv, st: (be[i], 0, 0)),
            pl.BlockSpec((1, de, d), lambda i, be, nv, st: (be[i], 0, 0)),
        ],
        out_specs=pl.BlockSpec((bm, d), lambda i, be, nv, st: (i, 0)),
        scratch_shapes=[pltpu.VMEM((2, bm, d), F32), pltpu.SemaphoreType.DMA((2,))],
    )
    return pl.pallas_call(
        functools.partial(_expert_kernel, bm=bm),
        grid_spec=grid_spec,
        out_shape=jax.ShapeDtypeStruct((nblk * bm, d), F32),
        compiler_params=pltpu.CompilerParams(
            dimension_semantics=("arbitrary",), vmem_limit_bytes=V7X_VMEM_LIMIT),
        name="experts",
    )(blk_e, nvalid, slot_t, b2d, wg_bf, wu_bf, wd_bf)


def _combine_kernel(dest_ref, h1_ref, rg_ref, y_hbm, out_ref, ybuf, sem):
    tm = h1_ref.shape[0]
    i = pl.program_id(0)
    n_steps = pl.num_programs(0)
    slot = i % 2

    def row_copy(tile, r, k, s):
        p = dest_ref[(tile * tm + r) * TOP_K + k]
        return pltpu.make_async_copy(y_hbm.at[pl.ds(p, 1)], ybuf.at[s, k, pl.ds(r, 1)], sem.at[s])

    def issue(tile, s):
        def body(r, c):
            for k in range(TOP_K):
                row_copy(tile, r, k, s).start()
            return c
        lax.fori_loop(0, tm, body, 0)

    @pl.when(i == 0)
    def _():
        issue(0, 0)

    @pl.when(i + 1 < n_steps)
    def _():
        issue(i + 1, 1 - slot)

    def wait_body(r, c):
        for k in range(TOP_K):
            row_copy(i, r, k, slot).wait()
        return c
    lax.fori_loop(0, tm, wait_body, 0)

    rg = rg_ref[...]
    out = h1_ref[...]
    for k in range(TOP_K):
        out = out + rg[:, k:k + 1] * ybuf[slot, k]
    out_ref[...] = out


def _combine(dest, h1, rg, y, tm):
    n, d = h1.shape
    row = lambda i, de: (i, 0)
    grid_spec = pltpu.PrefetchScalarGridSpec(
        num_scalar_prefetch=1,
        grid=(n // tm,),
        in_specs=[
            pl.BlockSpec((tm, d), row),
            pl.BlockSpec((tm, ROUTER_LANES), row),
            pl.BlockSpec(memory_space=pl.ANY),
        ],
        out_specs=pl.BlockSpec((tm, d), row),
        scratch_shapes=[pltpu.VMEM((2, TOP_K, tm, d), F32), pltpu.SemaphoreType.DMA((2,))],
    )
    return pl.pallas_call(
        _combine_kernel,
        grid_spec=grid_spec,
        out_shape=jax.ShapeDtypeStruct((n, d), F32),
        compiler_params=pltpu.CompilerParams(
            dimension_semantics=("arbitrary",), vmem_limit_bytes=V7X_VMEM_LIMIT),
        name="combine",
    )(dest, h1, rg, y)


def _rope_tables(seq):
    t = jnp.arange(seq)
    inv_freq = jnp.power(ROPE_THETA, -jnp.arange(0, ROPE_AXIS_DIM, 2, dtype=F32) / ROPE_AXIS_DIM)
    ang_r = (t // GRID_W).astype(F32)[:, None] * inv_freq[None, :]
    ang_c = (t % GRID_W).astype(F32)[:, None] * inv_freq[None, :]
    cr, sr, cc, sc = jnp.cos(ang_r), jnp.sin(ang_r), jnp.cos(ang_c), jnp.sin(ang_c)
    cos = jnp.concatenate([cr, cr, cc, cc], axis=-1)
    sin = jnp.concatenate([-sr, sr, -sc, sc], axis=-1)
    return cos, sin


def _routing_plan(expert, bm):
    n = expert.shape[0]
    m = n * TOP_K
    flat_e = expert.reshape(-1)
    order = jnp.argsort(flat_e).astype(jnp.int32)
    s_e = flat_e[order]
    counts = jnp.sum(flat_e[:, None] == jnp.arange(N_EXPERTS, dtype=jnp.int32)[None, :], axis=0,
                     dtype=jnp.int32)
    starts = jnp.cumsum(counts) - counts
    padded = (counts + bm - 1) // bm * bm
    pends = jnp.cumsum(padded)
    pstarts = pends - padded
    dest_sorted = (pstarts[s_e] + (jnp.arange(m, dtype=jnp.int32) - starts[s_e])).astype(jnp.int32)
    nblk = -(-m // bm) + N_EXPERTS
    slot_t = jnp.zeros((nblk * bm,), jnp.int32).at[dest_sorted].set(order // TOP_K)
    dest = jnp.zeros((m,), jnp.int32).at[order].set(dest_sorted)
    blk_start = jnp.arange(nblk, dtype=jnp.int32) * bm
    blk_e = jnp.minimum(jnp.sum(pends[None, :] <= blk_start[:, None], axis=1), N_EXPERTS - 1)
    nvalid = (pends[-1] // bm).reshape(1)
    return blk_e.astype(jnp.int32), nvalid.astype(jnp.int32), slot_t, dest


def kernel(x, meta_tokens, norm1_g, w_in, q_norm_g, k_norm_g, pool_w, pool_scale, w_out, norm2_g,
           w_router_group, b_router_group, w_router_expert, b_router_expert, w_gate, w_up, w_down):
    batch, seq, d = x.shape
    n = batch * seq
    assert norm1_g.shape[0] == 1, "single-layer block"
    assert seq % GRID_W == 0 and seq % 256 == 0

    tm_in = _tile(seq // 2, 512)
    tm_post = _tile(seq, 256)
    tq = _tile(seq, 256)
    bm = 256
    tm_comb = _tile(seq, 256)

    x2d = x.reshape(n, d)
    w_in_bf = w_in[0].astype(BF16)
    cos, sin = _rope_tables(seq)
    qg = q_norm_g[0].reshape(1, HEAD_DIM)
    kg = k_norm_g[0].reshape(1, HEAD_DIM)
    g1 = norm1_g[0].reshape(1, d)

    qt, k, vt, pp = _inproj(x2d, g1, w_in_bf, cos, sin, qg, kg, batch, seq, tm_in)
    ones = jnp.ones((N_META, HEAD_DIM), F32)
    _, km, vmt, pm = _inproj(meta_tokens.astype(F32), g1, w_in_bf, ones, jnp.zeros_like(ones),
                             qg, kg, 1, N_META, N_META)

    attn = _attention(qt, k, vt, km[0], vmt[0, :, 0], tq)

    wr = jnp.concatenate([w_router_expert[0], w_router_group[0],
                          jnp.zeros((d, ROUTER_LANES - N_EXPERTS - N_EXPERT_GROUPS), F32)], axis=1)
    br = jnp.concatenate([b_router_expert[0], b_router_group[0],
                          jnp.zeros((ROUTER_LANES - N_EXPERTS - N_EXPERT_GROUPS,), F32)])
    h1, b2d, ri, rg = _post(attn.reshape(n, ATTN_WIDTH), pp, pm, x2d, pool_w[0].astype(BF16),
                            pool_scale[0].reshape(1, POOL_WIDTH), w_out[0].astype(BF16),
                            norm2_g[0].reshape(1, d), wr.astype(BF16),
                            br.reshape(1, ROUTER_LANES), seq, tm_post)

    blk_e, nvalid, slot_t, dest = _routing_plan(ri[:, :TOP_K], bm)
    y = _experts(blk_e, nvalid, slot_t, b2d, w_gate[0].astype(BF16), w_up[0].astype(BF16),
                 w_down[0].astype(BF16), bm)
    out = _combine(dest, h1, rg, y, tm_comb)
    return out.reshape(batch, seq, d)
```

```python
import functools
import math

import jax
import jax.numpy as jnp
from jax import lax
from jax.experimental import pallas as pl
from jax.experimental.pallas import tpu as pltpu

N_META = 16
GRID_W = 64
HEAD_DIM = 128
N_Q_HEADS = 8
N_KV_HEADS = 2
Q_PER_KV = N_Q_HEADS // N_KV_HEADS
ATTN_WIDTH = N_Q_HEADS * HEAD_DIM
KV_WIDTH = N_KV_HEADS * HEAD_DIM
POOL_WINDOWS = (2, 4, 8, 16)
N_POOL_GROUPS = len(POOL_WINDOWS)
POOL_GROUP = 256
POOL_WIDTH = N_POOL_GROUPS * POOL_GROUP
ROPE_THETA = 10000.0
ROPE_AXIS_DIM = HEAD_DIM // 2
N_EXPERT_GROUPS = 4
EXPERTS_PER_GROUP = 8
N_EXPERTS = N_EXPERT_GROUPS * EXPERTS_PER_GROUP
TOP_K = 2
EPS = 1e-6

V_ONES_ROWS = 16
V_ROWS = HEAD_DIM + V_ONES_ROWS
HALO = 8
ROUTER_LANES = 128
GROUP_LANE0 = N_EXPERTS
DMA_UNROLL = 8
LOG2E = math.log2(math.e)
Q_SCALE = HEAD_DIM ** -0.5 * LOG2E

V7X_VMEM_LIMIT = 56 * 1024 * 1024

F32 = jnp.float32
BF16 = jnp.bfloat16


def _pack_bf16_pair(lo, hi):
    lo_bits = lax.bitcast_convert_type(lo.astype(BF16).astype(F32), jnp.uint32)
    hi_bits = lax.bitcast_convert_type(hi.astype(BF16).astype(F32), jnp.uint32)
    return (lo_bits >> 16) | hi_bits


def _unpack_bf16_pair(u):
    lo = lax.bitcast_convert_type(u << 16, F32)
    hi = lax.bitcast_convert_type(u & jnp.uint32(0xFFFF0000), F32)
    return lo, hi


def _tile(n, pref):
    t = min(n, pref)
    while n % t:
        t //= 2
    return t


def _inproj_kernel(x_ref, g1_ref, w_ref, cos_ref, sin_ref, qg_ref, kg_ref,
                   qt_ref, k_ref, vt_ref, p_ref):
    tm = x_ref.shape[0]
    xf = x_ref[...]
    ms = jnp.mean(xf * xf, axis=-1, keepdims=True)
    a = (xf * lax.rsqrt(ms + EPS) * g1_ref[...]).astype(BF16)
    cos = cos_ref[...]
    sin = sin_ref[...]
    lane = lax.broadcasted_iota(jnp.int32, (tm, HEAD_DIM), 1)
    first_half = (lane % ROPE_AXIS_DIM) < (ROPE_AXIS_DIM // 2)

    def norm_rope(hd, g, scale):
        n = hd * lax.rsqrt(jnp.mean(hd * hd, axis=-1, keepdims=True) + EPS) * g
        partner = jnp.where(first_half,
                            pltpu.roll(n, HEAD_DIM - ROPE_AXIS_DIM // 2, 1),
                            pltpu.roll(n, ROPE_AXIS_DIM // 2, 1))
        return (n * cos + partner * sin) * scale

    qg = qg_ref[...]
    kg = kg_ref[...]
    for c in range(N_Q_HEADS // 2):
        pr = jnp.dot(a, w_ref[:, c * 256:(c + 1) * 256], preferred_element_type=F32)
        for j in range(2):
            qh = norm_rope(pr[:, j * HEAD_DIM:(j + 1) * HEAD_DIM], qg, Q_SCALE)
            qt_ref[0, 2 * c + j] = qh.T.astype(BF16)
    pr = jnp.dot(a, w_ref[:, ATTN_WIDTH:ATTN_WIDTH + KV_WIDTH], preferred_element_type=F32)
    for j in range(N_KV_HEADS):
        k_ref[0, j] = norm_rope(pr[:, j * HEAD_DIM:(j + 1) * HEAD_DIM], kg, 1.0).astype(BF16)
    pr = jnp.dot(a, w_ref[:, ATTN_WIDTH + KV_WIDTH:ATTN_WIDTH + 2 * KV_WIDTH],
                 preferred_element_type=F32)
    for j in range(N_KV_HEADS):
        vt_ref[0, j, 0, 0:HEAD_DIM, :] = pr[:, j * HEAD_DIM:(j + 1) * HEAD_DIM].T.astype(BF16)
        vt_ref[0, j, 0, HEAD_DIM:V_ROWS, :] = jnp.ones((V_ONES_ROWS, tm), BF16)
    p_ref[...] = jnp.dot(a, w_ref[:, ATTN_WIDTH + 2 * KV_WIDTH:], preferred_element_type=F32)


def _inproj(x2d, g1, w_bf, cos, sin, qg, kg, batch, seq, tm):
    n, d = x2d.shape
    nj = seq // tm
    in_w = w_bf.shape[1]
    const = lambda i: (0, 0)
    return pl.pallas_call(
        _inproj_kernel,
        grid=(n // tm,),
        in_specs=[
            pl.BlockSpec((tm, d), lambda i: (i, 0)),
            pl.BlockSpec((1, d), const),
            pl.BlockSpec((d, in_w), const),
            pl.BlockSpec((tm, HEAD_DIM), lambda i: (i % nj, 0)),
            pl.BlockSpec((tm, HEAD_DIM), lambda i: (i % nj, 0)),
            pl.BlockSpec((1, HEAD_DIM), const),
            pl.BlockSpec((1, HEAD_DIM), const),
        ],
        out_specs=[
            pl.BlockSpec((1, N_Q_HEADS, HEAD_DIM, tm), lambda i: (i // nj, 0, 0, i % nj)),
            pl.BlockSpec((1, N_KV_HEADS, tm, HEAD_DIM), lambda i: (i // nj, 0, i % nj, 0)),
            pl.BlockSpec((1, N_KV_HEADS, 1, V_ROWS, tm), lambda i: (i // nj, 0, i % nj, 0, 0)),
            pl.BlockSpec((tm, POOL_WIDTH), lambda i: (i, 0)),
        ],
        out_shape=[
            jax.ShapeDtypeStruct((batch, N_Q_HEADS, HEAD_DIM, seq), BF16),
            jax.ShapeDtypeStruct((batch, N_KV_HEADS, seq, HEAD_DIM), BF16),
            jax.ShapeDtypeStruct((batch, N_KV_HEADS, nj, V_ROWS, tm), BF16),
            jax.ShapeDtypeStruct((n, POOL_WIDTH), F32),
        ],
        compiler_params=pltpu.CompilerParams(
            dimension_semantics=("arbitrary",), vmem_limit_bytes=V7X_VMEM_LIMIT),
        name="inproj",
    )(x2d, g1, w_bf, cos, sin, qg, kg)


def _attn_kernel(qt_ref, k_ref, vt_ref, km_ref, vmt_ref, o_ref, s_scr, acc_scr):
    tq = qt_ref.shape[3]
    nk = vt_ref.shape[2]
    tk = vt_ref.shape[4]
    unroll = next(u for u in (8, 4, 2) if nk % u == 0)
    qt = jnp.concatenate([qt_ref[0, g] for g in range(Q_PER_KV)], axis=1)

    def scores(j):
        off = pl.multiple_of(j * tk, tk)
        return jnp.dot(k_ref[0, 0, pl.ds(off, tk), :], qt, preferred_element_type=F32)

    s0 = jnp.dot(km_ref[0], qt, preferred_element_type=F32)
    m0 = jnp.max(s0, axis=0, keepdims=True)
    p0 = jnp.exp2(s0 - m0).astype(BF16)
    acc_scr[...] = jnp.dot(vmt_ref[0], p0, preferred_element_type=F32)
    s = scores(0)
    s_scr[0] = s
    mt0 = jnp.max(s, axis=0, keepdims=True)

    def step(j, cur, m, mt):
        s_next = scores(jnp.minimum(j + 1, nk - 1))
        s_scr[1 - cur] = s_next
        mt_next = jnp.max(s_next, axis=0, keepdims=True)
        m_new = jnp.maximum(m, mt)
        alpha = jnp.exp2(m - m_new)
        p = jnp.exp2(s_scr[cur] - m_new).astype(BF16)
        pv = jnp.dot(vt_ref[0, 0, j], p, preferred_element_type=F32)
        acc_scr[...] = alpha * acc_scr[...] + pv
        return m_new, mt_next

    def body(jb, carry):
        m, mt = carry
        for u in range(unroll):
            m, mt = step(unroll * jb + u, u % 2, m, mt)
        return m, mt

    lax.fori_loop(0, nk // unroll, body, (m0, mt0))
    out_t = acc_scr[0:HEAD_DIM, :] / acc_scr[HEAD_DIM:HEAD_DIM + 1, :]
    for g in range(Q_PER_KV):
        o_ref[0, :, g * HEAD_DIM:(g + 1) * HEAD_DIM] = out_t[:, g * tq:(g + 1) * tq].T.astype(BF16)


def _attention(qt, k, vt, km, vmt, tq):
    batch, _, _, seq = qt.shape
    nk, tk = vt.shape[2], vt.shape[4]
    assert nk % 2 == 0
    gw = Q_PER_KV * HEAD_DIM
    return pl.pallas_call(
        _attn_kernel,
        grid=(batch, N_KV_HEADS, seq // tq),
        in_specs=[
            pl.BlockSpec((1, Q_PER_KV, HEAD_DIM, tq), lambda b, h, i: (b, h, 0, i)),
            pl.BlockSpec((1, 1, seq, HEAD_DIM), lambda b, h, i: (b, h, 0, 0)),
            pl.BlockSpec((1, 1, nk, V_ROWS, tk), lambda b, h, i: (b, h, 0, 0, 0)),
            pl.BlockSpec((1, N_META, HEAD_DIM), lambda b, h, i: (h, 0, 0)),
            pl.BlockSpec((1, V_ROWS, N_META), lambda b, h, i: (h, 0, 0)),
        ],
        out_specs=pl.BlockSpec((1, tq, gw), lambda b, h, i: (b, i, h)),
        out_shape=jax.ShapeDtypeStruct((batch, seq, ATTN_WIDTH), BF16),
        scratch_shapes=[pltpu.VMEM((2, tk, Q_PER_KV * tq), F32),
                        pltpu.VMEM((V_ROWS, Q_PER_KV * tq), F32)],
        compiler_params=pltpu.CompilerParams(
            dimension_semantics=("arbitrary", "arbitrary", "arbitrary"),
            vmem_limit_bytes=V7X_VMEM_LIMIT),
        name="attention",
    )(qt, k, vt, km, vmt)


def _post_kernel(attn_ref, pp_ref, prev_ref, next_ref, pm_ref, x_ref, pw_ref, ps_ref, wo_ref,
                 g2_ref, wr_ref, br_ref, h1_ref, b_ref, ri_ref, rg_ref, cnt_ref, ext_ref, mix_ref,
                 *, nj, seq):
    tm = x_ref.shape[0]
    dh = x_ref.shape[1] // 2
    j = pl.program_id(0) % nj
    ext_ref[0:HALO] = jnp.where(j == 0, pm_ref[N_META - HALO:N_META, :], prev_ref[...])
    ext_ref[HALO:HALO + tm] = pp_ref[...]
    ext_ref[HALO + tm:2 * HALO + tm] = jnp.where(j == nj - 1, 0.0, next_ref[...])

    t = j * tm + lax.broadcasted_iota(jnp.int32, (tm, 1), 0)
    mix_ref[:, 0:ATTN_WIDTH] = attn_ref[...]
    for g, win in enumerate(POOL_WINDOWS):
        c0, c1 = g * POOL_GROUP, (g + 1) * POOL_GROUP
        half = win // 2
        tot = ext_ref[HALO - half:HALO - half + tm, c0:c1]
        for d in range(-half + 1, half):
            tot = tot + ext_ref[HALO + d:HALO + d + tm, c0:c1]
        cnt = (jnp.minimum(t + half, seq) - (t - half)).astype(F32)
        mg = (tot / cnt - pp_ref[:, c0:c1]).astype(BF16)
        yg = jnp.dot(mg, pw_ref[g], preferred_element_type=F32) * ps_ref[:, c0:c1]
        mix_ref[:, ATTN_WIDTH + c0:ATTN_WIDTH + c1] = yg.astype(BF16)

    h1 = x_ref[...] + jnp.dot(mix_ref[...], wo_ref[...], preferred_element_type=F32)
    h1_ref[...] = h1
    ms = jnp.mean(h1 * h1, axis=-1, keepdims=True)
    bf = h1 * lax.rsqrt(ms + EPS) * g2_ref[...]
    b_ref[...] = _pack_bf16_pair(bf[:, 0:dh], bf[:, dh:])
    logits = jnp.dot(bf.astype(BF16), wr_ref[...], preferred_element_type=F32) + br_ref[...]

    lane = lax.broadcasted_iota(jnp.int32, (tm, ROUTER_LANES), 1)
    lane_f = lane.astype(F32)
    neg = jnp.float32(-jnp.inf)
    big = jnp.float32(ROUTER_LANES)
    is_group = (lane >= GROUP_LANE0) & (lane < GROUP_LANE0 + N_EXPERT_GROUPS)
    gl = jnp.where(is_group, logits, neg)
    gmax = jnp.max(gl, axis=1, keepdims=True)
    gidx = jnp.min(jnp.where(gl == gmax, lane_f, big), axis=1, keepdims=True) - GROUP_LANE0
    g_p = 1.0 / jnp.sum(jnp.exp(gl - gmax), axis=1, keepdims=True)
    in_group = (lane // EXPERTS_PER_GROUP).astype(F32) == gidx
    el = jnp.where(in_group & (lane < N_EXPERTS), logits, neg)
    m1 = jnp.max(el, axis=1, keepdims=True)
    i1 = jnp.min(jnp.where(el == m1, lane_f, big), axis=1, keepdims=True)
    el2 = jnp.where(lane_f == i1, neg, el)
    m2 = jnp.max(el2, axis=1, keepdims=True)
    i2 = jnp.min(jnp.where(el2 == m2, lane_f, big), axis=1, keepdims=True)
    p2 = jnp.exp(m2 - m1)
    den = 1.0 + p2
    gate1 = g_p / den
    gate2 = g_p * p2 / den
    ri_ref[...] = jnp.where(lane == 0, i1, jnp.where(lane == 1, i2, 0.0)).astype(jnp.int32)
    rg_ref[...] = jnp.where(lane == 0, gate1, jnp.where(lane == 1, gate2, 0.0))

    @pl.when(pl.program_id(0) == 0)
    def _():
        cnt_ref[...] = jnp.zeros_like(cnt_ref)
    chosen = jnp.where(lane_f == i1, 1.0, 0.0) + jnp.where(lane_f == i2, 1.0, 0.0)
    cnt_ref[...] += jnp.broadcast_to(jnp.sum(chosen, axis=0, keepdims=True), cnt_ref.shape)


def _post(attn2d, pp, pm, x2d, pw_bf, ps, wo_bf, g2, wr_bf, br, seq, tm):
    n, d = x2d.shape
    nj = seq // tm
    hb = tm // HALO
    last_hb = n // HALO - 1
    const2 = lambda i: (0, 0)
    row = lambda i: (i, 0)
    return pl.pallas_call(
        functools.partial(_post_kernel, nj=nj, seq=seq),
        grid=(n // tm,),
        in_specs=[
            pl.BlockSpec((tm, ATTN_WIDTH), row),
            pl.BlockSpec((tm, POOL_WIDTH), row),
            pl.BlockSpec((HALO, POOL_WIDTH), lambda i: (jnp.maximum(i * hb - 1, 0), 0)),
            pl.BlockSpec((HALO, POOL_WIDTH), lambda i: (jnp.minimum((i + 1) * hb, last_hb), 0)),
            pl.BlockSpec((N_META, POOL_WIDTH), const2),
            pl.BlockSpec((tm, d), row),
            pl.BlockSpec((N_POOL_GROUPS, POOL_GROUP, POOL_GROUP), lambda i: (0, 0, 0)),
            pl.BlockSpec((1, POOL_WIDTH), const2),
            pl.BlockSpec((ATTN_WIDTH + POOL_WIDTH, d), const2),
            pl.BlockSpec((1, d), const2),
            pl.BlockSpec((d, ROUTER_LANES), const2),
            pl.BlockSpec((1, ROUTER_LANES), const2),
        ],
        out_specs=[
            pl.BlockSpec((tm, d), row),
            pl.BlockSpec((tm, d // 2), row),
            pl.BlockSpec((tm, ROUTER_LANES), row),
            pl.BlockSpec((tm, ROUTER_LANES), row),
            pl.BlockSpec((8, ROUTER_LANES), const2),
        ],
        out_shape=[
            jax.ShapeDtypeStruct((n, d), F32),
            jax.ShapeDtypeStruct((n, d // 2), jnp.uint32),
            jax.ShapeDtypeStruct((n, ROUTER_LANES), jnp.int32),
            jax.ShapeDtypeStruct((n, ROUTER_LANES), F32),
            jax.ShapeDtypeStruct((8, ROUTER_LANES), F32),
        ],
        scratch_shapes=[pltpu.VMEM((tm + 2 * HALO, POOL_WIDTH), F32),
                        pltpu.VMEM((tm, ATTN_WIDTH + POOL_WIDTH), BF16)],
        compiler_params=pltpu.CompilerParams(
            dimension_semantics=("arbitrary",), vmem_limit_bytes=V7X_VMEM_LIMIT),
        name="post",
    )(attn2d, pp, pp, pp, pm, x2d, pw_bf, ps, wo_bf, g2, wr_bf, br)


def _rank_kernel(ri_ref, cnt_ref, lower_ref, dest_ref, carry, pstart, *, bm):
    t = ri_ref.shape[0]

    @pl.when(pl.program_id(0) == 0)
    def _():
        counts = cnt_ref[...]
        padded = jnp.floor((counts + (bm - 1)) * (1.0 / bm)) * bm
        lane8 = lax.broadcasted_iota(jnp.int32, (8, ROUTER_LANES), 1)
        incl = padded
        shift = 1
        while shift < ROUTER_LANES:
            incl = incl + jnp.where(lane8 >= shift, pltpu.roll(incl, shift, 1), 0.0)
            shift *= 2
        pstart[...] = (incl - padded)[0:1]
        carry[...] = jnp.zeros_like(carry)

    lane = lax.broadcasted_iota(jnp.int32, (t, ROUTER_LANES), 1)
    ri = ri_ref[...]
    oh0 = lane == ri[:, 0:1]
    oh1 = lane == ri[:, 1:2]
    ohf = jnp.where(oh0, 1.0, 0.0) + jnp.where(oh1, 1.0, 0.0)
    prefix = jnp.dot(lower_ref[...], ohf.astype(BF16), preferred_element_type=F32)
    base = prefix + carry[...] + pstart[...]
    d0 = jnp.sum(jnp.where(oh0, base, 0.0), axis=1, keepdims=True)
    d1 = jnp.sum(jnp.where(oh1, base, 0.0), axis=1, keepdims=True)
    dest_ref[...] = jnp.where(lane == 0, d0, jnp.where(lane == 1, d1, 0.0)).astype(jnp.int32)
    carry[...] += jnp.sum(ohf, axis=0, keepdims=True)


def _rank(ri, cnt, bm, t):
    n = ri.shape[0]
    lower = jnp.tril(jnp.ones((t, t), BF16), -1)
    return pl.pallas_call(
        functools.partial(_rank_kernel, bm=bm),
        grid=(n // t,),
        in_specs=[
            pl.BlockSpec((t, ROUTER_LANES), lambda i: (i, 0)),
            pl.BlockSpec((8, ROUTER_LANES), lambda i: (0, 0)),
            pl.BlockSpec((t, t), lambda i: (0, 0)),
        ],
        out_specs=pl.BlockSpec((t, ROUTER_LANES), lambda i: (i, 0)),
        out_shape=jax.ShapeDtypeStruct((n, ROUTER_LANES), jnp.int32),
        scratch_shapes=[pltpu.VMEM((1, ROUTER_LANES), F32), pltpu.VMEM((1, ROUTER_LANES), F32)],
        compiler_params=pltpu.CompilerParams(
            dimension_semantics=("arbitrary",), vmem_limit_bytes=V7X_VMEM_LIMIT),
        name="rank",
    )(ri, cnt, lower)


def _dispatch_kernel(dest_ref, b_ref, xs_init_ref, xs_ref, sem):
    del xs_init_ref
    tm = b_ref.shape[0]
    i = pl.program_id(0)

    def row_copy(r, k):
        p = dest_ref[(i * tm + r) * TOP_K + k]
        return pltpu.make_async_copy(b_ref.at[pl.ds(r, 1)], xs_ref.at[pl.ds(p, 1)], sem)

    def issue(r8, c):
        for u in range(DMA_UNROLL):
            for k in range(TOP_K):
                row_copy(r8 * DMA_UNROLL + u, k).start()
        return c
    lax.fori_loop(0, tm // DMA_UNROLL, issue, 0)

    def drain(r8, c):
        for u in range(DMA_UNROLL):
            for k in range(TOP_K):
                row_copy(r8 * DMA_UNROLL + u, k).wait()
        return c
    lax.fori_loop(0, tm // DMA_UNROLL, drain, 0)


def _dispatch(dest, b_pk, n_slots, tm):
    n, hw = b_pk.shape
    xs_init = jnp.zeros((n_slots, hw), jnp.uint32)
    grid_spec = pltpu.PrefetchScalarGridSpec(
        num_scalar_prefetch=1,
        grid=(n // tm,),
        in_specs=[
            pl.BlockSpec((tm, hw), lambda i, de: (i, 0)),
            pl.BlockSpec(memory_space=pl.ANY),
        ],
        out_specs=pl.BlockSpec(memory_space=pl.ANY),
        scratch_shapes=[pltpu.SemaphoreType.DMA(())],
    )
    return pl.pallas_call(
        _dispatch_kernel,
        grid_spec=grid_spec,
        out_shape=jax.ShapeDtypeStruct((n_slots, hw), jnp.uint32),
        input_output_aliases={2: 0},
        compiler_params=pltpu.CompilerParams(
            dimension_semantics=("arbitrary",), vmem_limit_bytes=V7X_VMEM_LIMIT,
            has_side_effects=True),
        name="dispatch",
    )(dest, b_pk, xs_init)


def _expert_kernel(blk_e_ref, nvalid_ref, xs_ref, wg_ref, wu_ref, wd_ref, y_ref,
                   wg_bf, wu_bf, wd_bf):
    i = pl.program_id(0)
    nvalid = nvalid_ref[0]
    half = wd_ref.shape[2] // 2
    new_expert = jnp.logical_or(i == 0, blk_e_ref[i] != blk_e_ref[jnp.maximum(i - 1, 0)])

    @pl.when(jnp.logical_and(new_expert, i < nvalid))
    def _():
        wg_bf[...] = wg_ref[0].astype(BF16)
        wu_bf[...] = wu_ref[0].astype(BF16)
        wd_bf[...] = wd_ref[0].astype(BF16)

    @pl.when(i < nvalid)
    def _():
        lo, hi = _unpack_bf16_pair(xs_ref[...])
        xlo = lo.astype(BF16)
        xhi = hi.astype(BF16)
        hg = (jnp.dot(xlo, wg_bf[0:half, :], preferred_element_type=F32)
              + jnp.dot(xhi, wg_bf[half:, :], preferred_element_type=F32))
        hu = (jnp.dot(xlo, wu_bf[0:half, :], preferred_element_type=F32)
              + jnp.dot(xhi, wu_bf[half:, :], preferred_element_type=F32))
        h = (hg * jax.nn.sigmoid(hg) * hu).astype(BF16)
        y = jnp.dot(h, wd_bf[...], preferred_element_type=F32)
        y_ref[...] = _pack_bf16_pair(y[:, 0:half], y[:, half:])

    @pl.when(i >= nvalid)
    def _():
        y_ref[...] = jnp.zeros_like(y_ref)


def _experts(blk_e, nvalid, xs, w_gate, w_up, w_down, bm):
    n_slots, hw = xs.shape
    nblk = n_slots // bm
    _, d, de = w_gate.shape
    last = lambda i, nv: jnp.maximum(jnp.minimum(i, nv[0] - 1), 0)
    grid_spec = pltpu.PrefetchScalarGridSpec(
        num_scalar_prefetch=2,
        grid=(nblk,),
        in_specs=[
            pl.BlockSpec((bm, hw), lambda i, be, nv: (last(i, nv), 0)),
            pl.BlockSpec((1, d, de), lambda i, be, nv: (be[last(i, nv)], 0, 0)),
            pl.BlockSpec((1, d, de), lambda i, be, nv: (be[last(i, nv)], 0, 0)),
            pl.BlockSpec((1, de, d), lambda i, be, nv: (be[last(i, nv)], 0, 0)),
        ],
        out_specs=pl.BlockSpec((bm, hw), lambda i, be, nv: (i, 0)),
        scratch_shapes=[pltpu.VMEM((d, de), BF16), pltpu.VMEM((d, de), BF16),
                        pltpu.VMEM((de, d), BF16)],
    )
    return pl.pallas_call(
        _expert_kernel,
        grid_spec=grid_spec,
        out_shape=jax.ShapeDtypeStruct((n_slots, hw), jnp.uint32),
        compiler_params=pltpu.CompilerParams(
            dimension_semantics=("arbitrary",), vmem_limit_bytes=V7X_VMEM_LIMIT),
        name="experts",
    )(blk_e, nvalid, xs, w_gate, w_up, w_down)


def _combine_kernel(dest_ref, h1_ref, rg_ref, y_hbm, out_ref, ybuf, sem):
    tm = h1_ref.shape[0]
    half = ybuf.shape[3]
    i = pl.program_id(0)
    n_steps = pl.num_programs(0)
    slot = i % 2

    def row_copy(tile, r, k, s):
        p = dest_ref[(tile * tm + r) * TOP_K + k]
        return pltpu.make_async_copy(y_hbm.at[pl.ds(p, 1)], ybuf.at[s, k, pl.ds(r, 1)], sem.at[s])

    def issue(tile, s):
        def body(r8, c):
            for u in range(DMA_UNROLL):
                for k in range(TOP_K):
                    row_copy(tile, r8 * DMA_UNROLL + u, k, s).start()
            return c
        lax.fori_loop(0, tm // DMA_UNROLL, body, 0)

    @pl.when(i == 0)
    def _():
        issue(0, 0)

    @pl.when(i + 1 < n_steps)
    def _():
        issue(i + 1, 1 - slot)

    def drain(r8, c):
        for u in range(DMA_UNROLL):
            for k in range(TOP_K):
                row_copy(i, r8 * DMA_UNROLL + u, k, slot).wait()
        return c
    lax.fori_loop(0, tm // DMA_UNROLL, drain, 0)

    rg = rg_ref[...]
    lo = h1_ref[:, 0:half]
    hi = h1_ref[:, half:]
    for k in range(TOP_K):
        ylo, yhi = _unpack_bf16_pair(ybuf[slot, k])
        lo = lo + rg[:, k:k + 1] * ylo
        hi = hi + rg[:, k:k + 1] * yhi
    out_ref[:, 0:half] = lo
    out_ref[:, half:] = hi


def _combine(dest, h1, rg, y, tm):
    n, d = h1.shape
    hw = y.shape[1]
    row = lambda i, de: (i, 0)
    grid_spec = pltpu.PrefetchScalarGridSpec(
        num_scalar_prefetch=1,
        grid=(n // tm,),
        in_specs=[
            pl.BlockSpec((tm, d), row),
            pl.BlockSpec((tm, ROUTER_LANES), row),
            pl.BlockSpec(memory_space=pl.ANY),
        ],
        out_specs=pl.BlockSpec((tm, d), row),
        scratch_shapes=[pltpu.VMEM((2, TOP_K, tm, hw), jnp.uint32), pltpu.SemaphoreType.DMA((2,))],
    )
    return pl.pallas_call(
        _combine_kernel,
        grid_spec=grid_spec,
        out_shape=jax.ShapeDtypeStruct((n, d), F32),
        compiler_params=pltpu.CompilerParams(
            dimension_semantics=("arbitrary",), vmem_limit_bytes=V7X_VMEM_LIMIT),
        name="combine",
    )(dest, h1, rg, y)


def _rope_tables(seq):
    t = jnp.arange(seq)
    inv_freq = jnp.power(ROPE_THETA, -jnp.arange(0, ROPE_AXIS_DIM, 2, dtype=F32) / ROPE_AXIS_DIM)
    ang_r = (t // GRID_W).astype(F32)[:, None] * inv_freq[None, :]
    ang_c = (t % GRID_W).astype(F32)[:, None] * inv_freq[None, :]
    cr, sr, cc, sc = jnp.cos(ang_r), jnp.sin(ang_r), jnp.cos(ang_c), jnp.sin(ang_c)
    cos = jnp.concatenate([cr, cr, cc, cc], axis=-1)
    sin = jnp.concatenate([-sr, sr, -sc, sc], axis=-1)
    return cos, sin


def _block_plan(counts, nblk, bm):
    padded = (counts + bm - 1) // bm * bm
    pends = jnp.cumsum(padded)
    blk_start = jnp.arange(nblk, dtype=jnp.int32) * bm
    blk_e = jnp.minimum(jnp.sum(pends[None, :] <= blk_start[:, None], axis=1), N_EXPERTS - 1)
    nvalid = (pends[-1] // bm).reshape(1)
    return blk_e.astype(jnp.int32), nvalid.astype(jnp.int32)


def kernel(x, meta_tokens, norm1_g, w_in, q_norm_g, k_norm_g, pool_w, pool_scale, w_out, norm2_g,
           w_router_group, b_router_group, w_router_expert, b_router_expert, w_gate, w_up, w_down):
    batch, seq, d = x.shape
    n = batch * seq
    assert norm1_g.shape[0] == 1, "single-layer block"
    assert seq % GRID_W == 0 and seq % 256 == 0

    tm_in = _tile(seq // 2, 512)
    tm_post = _tile(seq, 256)
    tq = _tile(seq, 256)
    bm = 256
    tm_moe = _tile(seq, 256)

    x2d = x.reshape(n, d)
    w_in_bf = w_in[0].astype(BF16)
    cos, sin = _rope_tables(seq)
    qg = q_norm_g[0].reshape(1, HEAD_DIM)
    kg = k_norm_g[0].reshape(1, HEAD_DIM)
    g1 = norm1_g[0].reshape(1, d)

    qt, k, vt, pp = _inproj(x2d, g1, w_in_bf, cos, sin, qg, kg, batch, seq, tm_in)
    ones = jnp.ones((N_META, HEAD_DIM), F32)
    _, km, vmt, pm = _inproj(meta_tokens.astype(F32), g1, w_in_bf, ones, jnp.zeros_like(ones),
                             qg, kg, 1, N_META, N_META)

    attn = _attention(qt, k, vt, km[0], vmt[0, :, 0], tq)

    wr = jnp.concatenate([w_router_expert[0], w_router_group[0],
                          jnp.zeros((d, ROUTER_LANES - N_EXPERTS - N_EXPERT_GROUPS), F32)], axis=1)
    br = jnp.concatenate([b_router_expert[0], b_router_group[0],
                          jnp.zeros((ROUTER_LANES - N_EXPERTS - N_EXPERT_GROUPS,), F32)])
    h1, b_pk, ri, rg, cnt = _post(attn.reshape(n, ATTN_WIDTH), pp, pm, x2d, pool_w[0].astype(BF16),
                             pool_scale[0].reshape(1, POOL_WIDTH), w_out[0].astype(BF16),
                             norm2_g[0].reshape(1, d), wr.astype(BF16),
                             br.reshape(1, ROUTER_LANES), seq, tm_post)

    nblk = -(-(n * TOP_K) // bm) + N_EXPERTS
    dest = _rank(ri, cnt, bm, _tile(seq, 512))[:, :TOP_K].reshape(-1)
    blk_e, nvalid = _block_plan(cnt[0, :N_EXPERTS].astype(jnp.int32), nblk, bm)
    xs = _dispatch(dest, b_pk, nblk * bm, tm_moe)
    y = _experts(blk_e, nvalid, xs, w_gate[0], w_up[0], w_down[0], bm)
    out = _combine(dest, h1, rg, y, tm_moe)
    return out.reshape(batch, seq, d)
```

```python
import functools
import math

import jax
import jax.numpy as jnp
from jax import lax
from jax.experimental import pallas as pl
from jax.experimental.pallas import tpu as pltpu

N_META = 16
GRID_W = 64
HEAD_DIM = 128
N_Q_HEADS = 8
N_KV_HEADS = 2
Q_PER_KV = N_Q_HEADS // N_KV_HEADS
ATTN_WIDTH = N_Q_HEADS * HEAD_DIM
KV_WIDTH = N_KV_HEADS * HEAD_DIM
POOL_WINDOWS = (2, 4, 8, 16)
N_POOL_GROUPS = len(POOL_WINDOWS)
POOL_GROUP = 256
POOL_WIDTH = N_POOL_GROUPS * POOL_GROUP
ROPE_THETA = 10000.0
ROPE_AXIS_DIM = HEAD_DIM // 2
N_EXPERT_GROUPS = 4
EXPERTS_PER_GROUP = 8
N_EXPERTS = N_EXPERT_GROUPS * EXPERTS_PER_GROUP
TOP_K = 2
EPS = 1e-6

V_ONES_ROWS = 16
V_ROWS = HEAD_DIM + V_ONES_ROWS
HALO = 8
ROUTER_LANES = 128
GROUP_LANE0 = N_EXPERTS
DMA_UNROLL = 8
LOG2E = math.log2(math.e)
Q_SCALE = HEAD_DIM ** -0.5 * LOG2E

V7X_VMEM_LIMIT = 56 * 1024 * 1024

F32 = jnp.float32
BF16 = jnp.bfloat16


def _pack_bf16_pair(lo, hi):
    lo_bits = lax.bitcast_convert_type(lo.astype(BF16).astype(F32), jnp.uint32)
    hi_bits = lax.bitcast_convert_type(hi.astype(BF16).astype(F32), jnp.uint32)
    return (lo_bits >> 16) | hi_bits


def _unpack_bf16_pair(u):
    lo = lax.bitcast_convert_type(u << 16, F32)
    hi = lax.bitcast_convert_type(u & jnp.uint32(0xFFFF0000), F32)
    return lo, hi


def _tile(n, pref):
    t = min(n, pref)
    while n % t:
        t //= 2
    return t


def _inproj_kernel(x_ref, g1_ref, w_ref, cos_ref, sin_ref, qg_ref, kg_ref,
                   qt_ref, k_ref, vt_ref, p_ref):
    tm = x_ref.shape[0]
    xf = x_ref[...]
    ms = jnp.mean(xf * xf, axis=-1, keepdims=True)
    a = (xf * lax.rsqrt(ms + EPS) * g1_ref[...]).astype(BF16)
    cos = cos_ref[...]
    sin = sin_ref[...]
    lane = lax.broadcasted_iota(jnp.int32, (tm, HEAD_DIM), 1)
    first_half = (lane % ROPE_AXIS_DIM) < (ROPE_AXIS_DIM // 2)

    def norm_rope(hd, g, scale):
        n = hd * lax.rsqrt(jnp.mean(hd * hd, axis=-1, keepdims=True) + EPS) * g
        partner = jnp.where(first_half,
                            pltpu.roll(n, HEAD_DIM - ROPE_AXIS_DIM // 2, 1),
                            pltpu.roll(n, ROPE_AXIS_DIM // 2, 1))
        return (n * cos + partner * sin) * scale

    qg = qg_ref[...]
    kg = kg_ref[...]
    for c in range(N_Q_HEADS // 2):
        pr = jnp.dot(a, w_ref[:, c * 256:(c + 1) * 256], preferred_element_type=F32)
        for j in range(2):
            qh = norm_rope(pr[:, j * HEAD_DIM:(j + 1) * HEAD_DIM], qg, Q_SCALE)
            qt_ref[0, 2 * c + j] = qh.T.astype(BF16)
    pr = jnp.dot(a, w_ref[:, ATTN_WIDTH:ATTN_WIDTH + KV_WIDTH], preferred_element_type=F32)
    for j in range(N_KV_HEADS):
        k_ref[0, j] = norm_rope(pr[:, j * HEAD_DIM:(j + 1) * HEAD_DIM], kg, 1.0).astype(BF16)
    pr = jnp.dot(a, w_ref[:, ATTN_WIDTH + KV_WIDTH:ATTN_WIDTH + 2 * KV_WIDTH],
                 preferred_element_type=F32)
    for j in range(N_KV_HEADS):
        vt_ref[0, j, 0, 0:HEAD_DIM, :] = pr[:, j * HEAD_DIM:(j + 1) * HEAD_DIM].T.astype(BF16)
        vt_ref[0, j, 0, HEAD_DIM:V_ROWS, :] = jnp.ones((V_ONES_ROWS, tm), BF16)
    p_ref[...] = jnp.dot(a, w_ref[:, ATTN_WIDTH + 2 * KV_WIDTH:], preferred_element_type=F32)


def _inproj(x2d, g1, w_bf, cos, sin, qg, kg, batch, seq, tm):
    n, d = x2d.shape
    nj = seq // tm
    in_w = w_bf.shape[1]
    const = lambda i: (0, 0)
    return pl.pallas_call(
        _inproj_kernel,
        grid=(n // tm,),
        in_specs=[
            pl.BlockSpec((tm, d), lambda i: (i, 0)),
            pl.BlockSpec((1, d), const),
            pl.BlockSpec((d, in_w), const),
            pl.BlockSpec((tm, HEAD_DIM), lambda i: (i % nj, 0)),
            pl.BlockSpec((tm, HEAD_DIM), lambda i: (i % nj, 0)),
            pl.BlockSpec((1, HEAD_DIM), const),
            pl.BlockSpec((1, HEAD_DIM), const),
        ],
        out_specs=[
            pl.BlockSpec((1, N_Q_HEADS, HEAD_DIM, tm), lambda i: (i // nj, 0, 0, i % nj)),
            pl.BlockSpec((1, N_KV_HEADS, tm, HEAD_DIM), lambda i: (i // nj, 0, i % nj, 0)),
            pl.BlockSpec((1, N_KV_HEADS, 1, V_ROWS, tm), lambda i: (i // nj, 0, i % nj, 0, 0)),
            pl.BlockSpec((tm, POOL_WIDTH), lambda i: (i, 0)),
        ],
        out_shape=[
            jax.ShapeDtypeStruct((batch, N_Q_HEADS, HEAD_DIM, seq), BF16),
            jax.ShapeDtypeStruct((batch, N_KV_HEADS, seq, HEAD_DIM), BF16),
            jax.ShapeDtypeStruct((batch, N_KV_HEADS, nj, V_ROWS, tm), BF16),
            jax.ShapeDtypeStruct((n, POOL_WIDTH), F32),
        ],
        compiler_params=pltpu.CompilerParams(
            dimension_semantics=("arbitrary",), vmem_limit_bytes=V7X_VMEM_LIMIT),
        name="inproj",
    )(x2d, g1, w_bf, cos, sin, qg, kg)


def _attn_kernel(qt_ref, k_ref, vt_ref, km_ref, vmt_ref, o_ref, s_scr, acc_scr):
    tq = qt_ref.shape[3]
    nk = vt_ref.shape[2]
    tk = vt_ref.shape[4]
    unroll = next(u for u in (8, 4, 2) if nk % u == 0)
    qt = jnp.concatenate([qt_ref[0, g] for g in range(Q_PER_KV)], axis=1)

    def scores(j):
        off = pl.multiple_of(j * tk, tk)
        return jnp.dot(k_ref[0, 0, pl.ds(off, tk), :], qt, preferred_element_type=F32)

    s0 = jnp.dot(km_ref[0], qt, preferred_element_type=F32)
    m0 = jnp.max(s0, axis=0, keepdims=True)
    p0 = jnp.exp2(s0 - m0).astype(BF16)
    acc_scr[...] = jnp.dot(vmt_ref[0], p0, preferred_element_type=F32)
    s = scores(0)
    s_scr[0] = s
    mt0 = jnp.max(s, axis=0, keepdims=True)

    def step(j, cur, m, mt):
        s_next = scores(jnp.minimum(j + 1, nk - 1))
        s_scr[1 - cur] = s_next
        mt_next = jnp.max(s_next, axis=0, keepdims=True)
        m_new = jnp.maximum(m, mt)
        alpha = jnp.exp2(m - m_new)
        p = jnp.exp2(s_scr[cur] - m_new).astype(BF16)
        pv = jnp.dot(vt_ref[0, 0, j], p, preferred_element_type=F32)
        acc_scr[...] = alpha * acc_scr[...] + pv
        return m_new, mt_next

    def body(jb, carry):
        m, mt = carry
        for u in range(unroll):
            m, mt = step(unroll * jb + u, u % 2, m, mt)
        return m, mt

    lax.fori_loop(0, nk // unroll, body, (m0, mt0))
    out_t = acc_scr[0:HEAD_DIM, :] / acc_scr[HEAD_DIM:HEAD_DIM + 1, :]
    for g in range(Q_PER_KV):
        o_ref[0, :, g * HEAD_DIM:(g + 1) * HEAD_DIM] = out_t[:, g * tq:(g + 1) * tq].T.astype(BF16)


def _attention(qt, k, vt, km, vmt, tq):
    batch, _, _, seq = qt.shape
    nk, tk = vt.shape[2], vt.shape[4]
    assert nk % 2 == 0
    gw = Q_PER_KV * HEAD_DIM
    return pl.pallas_call(
        _attn_kernel,
        grid=(batch, N_KV_HEADS, seq // tq),
        in_specs=[
            pl.BlockSpec((1, Q_PER_KV, HEAD_DIM, tq), lambda b, h, i: (b, h, 0, i)),
            pl.BlockSpec((1, 1, seq, HEAD_DIM), lambda b, h, i: (b, h, 0, 0)),
            pl.BlockSpec((1, 1, nk, V_ROWS, tk), lambda b, h, i: (b, h, 0, 0, 0)),
            pl.BlockSpec((1, N_META, HEAD_DIM), lambda b, h, i: (h, 0, 0)),
            pl.BlockSpec((1, V_ROWS, N_META), lambda b, h, i: (h, 0, 0)),
        ],
        out_specs=pl.BlockSpec((1, tq, gw), lambda b, h, i: (b, i, h)),
        out_shape=jax.ShapeDtypeStruct((batch, seq, ATTN_WIDTH), BF16),
        scratch_shapes=[pltpu.VMEM((2, tk, Q_PER_KV * tq), F32),
                        pltpu.VMEM((V_ROWS, Q_PER_KV * tq), F32)],
        compiler_params=pltpu.CompilerParams(
            dimension_semantics=("arbitrary", "arbitrary", "arbitrary"),
            vmem_limit_bytes=V7X_VMEM_LIMIT),
        name="attention",
    )(qt, k, vt, km, vmt)


def _post_kernel(attn_ref, pp_ref, prev_ref, next_ref, pm_ref, x_ref, pw_ref, ps_ref, wo_ref,
                 g2_ref, wr_ref, br_ref, h1_ref, b_ref, ri_ref, rg_ref, cnt_ref, ext_ref, mix_ref,
                 *, nj, seq, nsub):
    tm = x_ref.shape[0]
    ts = tm // nsub
    dh = x_ref.shape[1] // 2
    j = pl.program_id(0) % nj
    ext_ref[0:HALO] = jnp.where(j == 0, pm_ref[N_META - HALO:N_META, :], prev_ref[...])
    ext_ref[HALO:HALO + tm] = pp_ref[...]
    ext_ref[HALO + tm:2 * HALO + tm] = jnp.where(j == nj - 1, 0.0, next_ref[...])

    @pl.when(pl.program_id(0) == 0)
    def _():
        cnt_ref[...] = jnp.zeros_like(cnt_ref)

    lane = lax.broadcasted_iota(jnp.int32, (ts, ROUTER_LANES), 1)
    lane_f = lane.astype(F32)
    neg = jnp.float32(-jnp.inf)
    big = jnp.float32(ROUTER_LANES)
    is_group = (lane >= GROUP_LANE0) & (lane < GROUP_LANE0 + N_EXPERT_GROUPS)

    for s in range(nsub):
        r0 = s * ts
        rows = slice(r0, r0 + ts)
        mix_ref[rows, 0:ATTN_WIDTH] = attn_ref[rows, :]
        for g, win in enumerate(POOL_WINDOWS):
            c0, c1 = g * POOL_GROUP, (g + 1) * POOL_GROUP
            half = win // 2
            ext_rows = ts + 2 * HALO
            e = ext_ref[r0:r0 + ext_rows, c0:c1]
            ahead = lambda a, k: pltpu.roll(a, ext_rows - k, 0)
            behind = lambda a, k: pltpu.roll(a, k, 0)
            fwd = e
            k = 1
            while k < half:
                fwd = fwd + ahead(fwd, k)
                k *= 2
            tot = (fwd + behind(fwd, half))[HALO:HALO + ts]
            if s == nsub - 1:
                t8 = j * tm + (tm - HALO) + lax.broadcasted_iota(jnp.int32, (HALO, 1), 0)
                cnt8 = (jnp.minimum(t8 + half, seq) - (t8 - half)).astype(F32)
                mean = jnp.concatenate([tot[0:ts - HALO] * (1.0 / win), tot[ts - HALO:ts] / cnt8], axis=0)
            else:
                mean = tot * (1.0 / win)
            mg = (mean - pp_ref[rows, c0:c1]).astype(BF16)
            yg = jnp.dot(mg, pw_ref[g], preferred_element_type=F32) * ps_ref[:, c0:c1]
            mix_ref[rows, ATTN_WIDTH + c0:ATTN_WIDTH + c1] = yg.astype(BF16)

        h1 = x_ref[rows, :] + jnp.dot(mix_ref[rows, :], wo_ref[...], preferred_element_type=F32)
        h1_ref[rows, :] = h1
        ms = jnp.mean(h1 * h1, axis=-1, keepdims=True)
        bf = h1 * lax.rsqrt(ms + EPS) * g2_ref[...]
        b_ref[rows, :] = _pack_bf16_pair(bf[:, 0:dh], bf[:, dh:])
        logits = jnp.dot(bf.astype(BF16), wr_ref[...], preferred_element_type=F32) + br_ref[...]

        gl = jnp.where(is_group, logits, neg)
        gmax = jnp.max(gl, axis=1, keepdims=True)
        gidx = jnp.min(jnp.where(gl == gmax, lane_f, big), axis=1, keepdims=True) - GROUP_LANE0
        g_p = 1.0 / jnp.sum(jnp.exp(gl - gmax), axis=1, keepdims=True)
        in_group = (lane // EXPERTS_PER_GROUP).astype(F32) == gidx
        el = jnp.where(in_group & (lane < N_EXPERTS), logits, neg)
        m1 = jnp.max(el, axis=1, keepdims=True)
        i1 = jnp.min(jnp.where(el == m1, lane_f, big), axis=1, keepdims=True)
        el2 = jnp.where(lane_f == i1, neg, el)
        m2 = jnp.max(el2, axis=1, keepdims=True)
        i2 = jnp.min(jnp.where(el2 == m2, lane_f, big), axis=1, keepdims=True)
        p2 = jnp.exp(m2 - m1)
        den = 1.0 + p2
        gate1 = g_p / den
        gate2 = g_p * p2 / den
        ri_ref[rows, :] = jnp.where(lane == 0, i1, jnp.where(lane == 1, i2, 0.0)).astype(jnp.int32)
        rg_ref[rows, :] = jnp.where(lane == 0, gate1, jnp.where(lane == 1, gate2, 0.0))

        chosen = jnp.where(lane_f == i1, 1.0, 0.0) + jnp.where(lane_f == i2, 1.0, 0.0)
        cnt_ref[...] += jnp.broadcast_to(jnp.sum(chosen, axis=0, keepdims=True), cnt_ref.shape)


def _post(attn2d, pp, pm, x2d, pw_bf, ps, wo_bf, g2, wr_bf, br, seq, tm, nsub):
    n, d = x2d.shape
    nj = seq // tm
    hb = tm // HALO
    last_hb = n // HALO - 1
    const2 = lambda i: (0, 0)
    row = lambda i: (i, 0)
    return pl.pallas_call(
        functools.partial(_post_kernel, nj=nj, seq=seq, nsub=nsub),
        grid=(n // tm,),
        in_specs=[
            pl.BlockSpec((tm, ATTN_WIDTH), row),
            pl.BlockSpec((tm, POOL_WIDTH), row),
            pl.BlockSpec((HALO, POOL_WIDTH), lambda i: (jnp.maximum(i * hb - 1, 0), 0)),
            pl.BlockSpec((HALO, POOL_WIDTH), lambda i: (jnp.minimum((i + 1) * hb, last_hb), 0)),
            pl.BlockSpec((N_META, POOL_WIDTH), const2),
            pl.BlockSpec((tm, d), row),
            pl.BlockSpec((N_POOL_GROUPS, POOL_GROUP, POOL_GROUP), lambda i: (0, 0, 0)),
            pl.BlockSpec((1, POOL_WIDTH), const2),
            pl.BlockSpec((ATTN_WIDTH + POOL_WIDTH, d), const2),
            pl.BlockSpec((1, d), const2),
            pl.BlockSpec((d, ROUTER_LANES), const2),
            pl.BlockSpec((1, ROUTER_LANES), const2),
        ],
        out_specs=[
            pl.BlockSpec((tm, d), row),
            pl.BlockSpec((tm, d // 2), row),
            pl.BlockSpec((tm, ROUTER_LANES), row),
            pl.BlockSpec((tm, ROUTER_LANES), row),
            pl.BlockSpec((8, ROUTER_LANES), const2),
        ],
        out_shape=[
            jax.ShapeDtypeStruct((n, d), F32),
            jax.ShapeDtypeStruct((n, d // 2), jnp.uint32),
            jax.ShapeDtypeStruct((n, ROUTER_LANES), jnp.int32),
            jax.ShapeDtypeStruct((n, ROUTER_LANES), F32),
            jax.ShapeDtypeStruct((8, ROUTER_LANES), F32),
        ],
        scratch_shapes=[pltpu.VMEM((tm + 2 * HALO, POOL_WIDTH), F32),
                        pltpu.VMEM((tm, ATTN_WIDTH + POOL_WIDTH), BF16)],
        compiler_params=pltpu.CompilerParams(
            dimension_semantics=("arbitrary",), vmem_limit_bytes=V7X_VMEM_LIMIT),
        name="post",
    )(attn2d, pp, pp, pp, pm, x2d, pw_bf, ps, wo_bf, g2, wr_bf, br)


def _rank_kernel(ri_ref, cnt_ref, lower_ref, dest_ref, carry, pstart, *, bm):
    t = ri_ref.shape[0]

    @pl.when(pl.program_id(0) == 0)
    def _():
        counts = cnt_ref[...]
        padded = jnp.floor((counts + (bm - 1)) * (1.0 / bm)) * bm
        lane8 = lax.broadcasted_iota(jnp.int32, (8, ROUTER_LANES), 1)
        incl = padded
        shift = 1
        while shift < ROUTER_LANES:
            incl = incl + jnp.where(lane8 >= shift, pltpu.roll(incl, shift, 1), 0.0)
            shift *= 2
        pstart[...] = (incl - padded)[0:1]
        carry[...] = jnp.zeros_like(carry)

    lane = lax.broadcasted_iota(jnp.int32, (t, ROUTER_LANES), 1)
    ri = ri_ref[...]
    oh0 = lane == ri[:, 0:1]
    oh1 = lane == ri[:, 1:2]
    ohf = jnp.where(oh0, 1.0, 0.0) + jnp.where(oh1, 1.0, 0.0)
    prefix = jnp.dot(lower_ref[...], ohf.astype(BF16), preferred_element_type=F32)
    base = prefix + carry[...] + pstart[...]
    d0 = jnp.sum(jnp.where(oh0, base, 0.0), axis=1, keepdims=True)
    d1 = jnp.sum(jnp.where(oh1, base, 0.0), axis=1, keepdims=True)
    dest_ref[...] = jnp.where(lane == 0, d0, jnp.where(lane == 1, d1, 0.0)).astype(jnp.int32)
    carry[...] += jnp.sum(ohf, axis=0, keepdims=True)


def _rank(ri, cnt, bm, t):
    n = ri.shape[0]
    lower = jnp.tril(jnp.ones((t, t), BF16), -1)
    return pl.pallas_call(
        functools.partial(_rank_kernel, bm=bm),
        grid=(n // t,),
        in_specs=[
            pl.BlockSpec((t, ROUTER_LANES), lambda i: (i, 0)),
            pl.BlockSpec((8, ROUTER_LANES), lambda i: (0, 0)),
            pl.BlockSpec((t, t), lambda i: (0, 0)),
        ],
        out_specs=pl.BlockSpec((t, ROUTER_LANES), lambda i: (i, 0)),
        out_shape=jax.ShapeDtypeStruct((n, ROUTER_LANES), jnp.int32),
        scratch_shapes=[pltpu.VMEM((1, ROUTER_LANES), F32), pltpu.VMEM((1, ROUTER_LANES), F32)],
        compiler_params=pltpu.CompilerParams(
            dimension_semantics=("arbitrary",), vmem_limit_bytes=V7X_VMEM_LIMIT),
        name="rank",
    )(ri, cnt, lower)


def _dispatch_kernel(dest_ref, pad0_ref, padn_ref, nvalid_ref, b_ref, xs_ref, zeros, sem, zsem, *, bm):
    tm = b_ref.shape[0]
    i = pl.program_id(0)

    @pl.when(i == 0)
    def _():
        zeros[...] = jnp.zeros_like(zeros)
        nblk = xs_ref.shape[0] // bm

        def pad_copy(e, r):
            return pltpu.make_async_copy(zeros.at[pl.ds(0, 1)], xs_ref.at[pl.ds(pad0_ref[e] + r, 1)], zsem)

        def tail_copy(blk):
            return pltpu.make_async_copy(zeros, xs_ref.at[pl.ds(pl.multiple_of(blk * bm, bm), bm)], zsem)

        def per_expert(e, c):
            lax.fori_loop(0, padn_ref[e], lambda r, c2: (pad_copy(e, r).start(), c2)[1], 0)
            return c
        lax.fori_loop(0, N_EXPERTS, per_expert, 0)
        lax.fori_loop(nvalid_ref[0], nblk, lambda blk, c: (tail_copy(blk).start(), c)[1], 0)

        def per_expert_wait(e, c):
            lax.fori_loop(0, padn_ref[e], lambda r, c2: (pad_copy(e, r).wait(), c2)[1], 0)
            return c
        lax.fori_loop(0, N_EXPERTS, per_expert_wait, 0)
        lax.fori_loop(nvalid_ref[0], nblk, lambda blk, c: (tail_copy(blk).wait(), c)[1], 0)

    def row_copy(r, k):
        p = dest_ref[(i * tm + r) * TOP_K + k]
        return pltpu.make_async_copy(b_ref.at[pl.ds(r, 1)], xs_ref.at[pl.ds(p, 1)], sem)

    def issue(r8, c):
        for u in range(DMA_UNROLL):
            for k in range(TOP_K):
                row_copy(r8 * DMA_UNROLL + u, k).start()
        return c
    lax.fori_loop(0, tm // DMA_UNROLL, issue, 0)

    def drain(r8, c):
        for u in range(DMA_UNROLL):
            for k in range(TOP_K):
                row_copy(r8 * DMA_UNROLL + u, k).wait()
        return c
    lax.fori_loop(0, tm // DMA_UNROLL, drain, 0)


def _dispatch(dest, pad0, padn, nvalid, b_pk, n_slots, tm, bm):
    n, hw = b_pk.shape
    grid_spec = pltpu.PrefetchScalarGridSpec(
        num_scalar_prefetch=4,
        grid=(n // tm,),
        in_specs=[pl.BlockSpec((tm, hw), lambda i, *_: (i, 0))],
        out_specs=pl.BlockSpec(memory_space=pl.ANY),
        scratch_shapes=[pltpu.VMEM((bm, hw), jnp.uint32), pltpu.SemaphoreType.DMA(()),
                        pltpu.SemaphoreType.DMA(())],
    )
    return pl.pallas_call(
        functools.partial(_dispatch_kernel, bm=bm),
        grid_spec=grid_spec,
        out_shape=jax.ShapeDtypeStruct((n_slots, hw), jnp.uint32),
        compiler_params=pltpu.CompilerParams(
            dimension_semantics=("arbitrary",), vmem_limit_bytes=V7X_VMEM_LIMIT,
            has_side_effects=True),
        name="dispatch",
    )(dest, pad0, padn, nvalid, b_pk)


def _expert_kernel(blk_e_ref, nvalid_ref, xs_ref, wg_ref, wu_ref, wd_ref, y_ref,
                   wg_bf, wu_bf, wd_bf):
    i = pl.program_id(0)
    nvalid = nvalid_ref[0]
    half = wd_ref.shape[2] // 2
    new_expert = jnp.logical_or(i == 0, blk_e_ref[i] != blk_e_ref[jnp.maximum(i - 1, 0)])

    @pl.when(jnp.logical_and(new_expert, i < nvalid))
    def _():
        wg_bf[...] = wg_ref[0].astype(BF16)
        wu_bf[...] = wu_ref[0].astype(BF16)
        wd_bf[...] = wd_ref[0].astype(BF16)

    @pl.when(i < nvalid)
    def _():
        lo, hi = _unpack_bf16_pair(xs_ref[...])
        xlo = lo.astype(BF16)
        xhi = hi.astype(BF16)
        hg = (jnp.dot(xlo, wg_bf[0:half, :], preferred_element_type=F32)
              + jnp.dot(xhi, wg_bf[half:, :], preferred_element_type=F32))
        hu = (jnp.dot(xlo, wu_bf[0:half, :], preferred_element_type=F32)
              + jnp.dot(xhi, wu_bf[half:, :], preferred_element_type=F32))
        h = (hg * jax.nn.sigmoid(hg) * hu).astype(BF16)
        y = jnp.dot(h, wd_bf[...], preferred_element_type=F32)
        y_ref[...] = _pack_bf16_pair(y[:, 0:half], y[:, half:])

    @pl.when(i >= nvalid)
    def _():
        y_ref[...] = jnp.zeros_like(y_ref)


def _experts(blk_e, nvalid, xs, w_gate, w_up, w_down, bm):
    n_slots, hw = xs.shape
    nblk = n_slots // bm
    _, d, de = w_gate.shape
    last = lambda i, nv: jnp.maximum(jnp.minimum(i, nv[0] - 1), 0)
    grid_spec = pltpu.PrefetchScalarGridSpec(
        num_scalar_prefetch=2,
        grid=(nblk,),
        in_specs=[
            pl.BlockSpec((bm, hw), lambda i, be, nv: (last(i, nv), 0)),
            pl.BlockSpec((1, d, de), lambda i, be, nv: (be[last(i, nv)], 0, 0)),
            pl.BlockSpec((1, d, de), lambda i, be, nv: (be[last(i, nv)], 0, 0)),
            pl.BlockSpec((1, de, d), lambda i, be, nv: (be[last(i, nv)], 0, 0)),
        ],
        out_specs=pl.BlockSpec((bm, hw), lambda i, be, nv: (i, 0)),
        scratch_shapes=[pltpu.VMEM((d, de), BF16), pltpu.VMEM((d, de), BF16),
                        pltpu.VMEM((de, d), BF16)],
    )
    return pl.pallas_call(
        _expert_kernel,
        grid_spec=grid_spec,
        out_shape=jax.ShapeDtypeStruct((n_slots, hw), jnp.uint32),
        compiler_params=pltpu.CompilerParams(
            dimension_semantics=("arbitrary",), vmem_limit_bytes=V7X_VMEM_LIMIT),
        name="experts",
    )(blk_e, nvalid, xs, w_gate, w_up, w_down)


def _combine_kernel(dest_ref, h1_ref, rg_ref, y_hbm, out_ref, ybuf, sem):
    tm = h1_ref.shape[0]
    half = ybuf.shape[3]
    i = pl.program_id(0)
    n_steps = pl.num_programs(0)
    slot = i % 2

    def row_copy(tile, r, k, s):
        p = dest_ref[(tile * tm + r) * TOP_K + k]
        return pltpu.make_async_copy(y_hbm.at[pl.ds(p, 1)], ybuf.at[s, k, pl.ds(r, 1)], sem.at[s])

    def issue(tile, s):
        def body(r8, c):
            for u in range(DMA_UNROLL):
                for k in range(TOP_K):
                    row_copy(tile, r8 * DMA_UNROLL + u, k, s).start()
            return c
        lax.fori_loop(0, tm // DMA_UNROLL, body, 0)

    @pl.when(i == 0)
    def _():
        issue(0, 0)

    @pl.when(i + 1 < n_steps)
    def _():
        issue(i + 1, 1 - slot)

    def drain(r8, c):
        for u in range(DMA_UNROLL):
            for k in range(TOP_K):
                row_copy(i, r8 * DMA_UNROLL + u, k, slot).wait()
        return c
    lax.fori_loop(0, tm // DMA_UNROLL, drain, 0)

    rg = rg_ref[...]
    lo = h1_ref[:, 0:half]
    hi = h1_ref[:, half:]
    for k in range(TOP_K):
        ylo, yhi = _unpack_bf16_pair(ybuf[slot, k])
        lo = lo + rg[:, k:k + 1] * ylo
        hi = hi + rg[:, k:k + 1] * yhi
    out_ref[:, 0:half] = lo
    out_ref[:, half:] = hi


def _combine(dest, h1, rg, y, tm):
    n, d = h1.shape
    hw = y.shape[1]
    row = lambda i, de: (i, 0)
    grid_spec = pltpu.PrefetchScalarGridSpec(
        num_scalar_prefetch=1,
        grid=(n // tm,),
        in_specs=[
            pl.BlockSpec((tm, d), row),
            pl.BlockSpec((tm, ROUTER_LANES), row),
            pl.BlockSpec(memory_space=pl.ANY),
        ],
        out_specs=pl.BlockSpec((tm, d), row),
        scratch_shapes=[pltpu.VMEM((2, TOP_K, tm, hw), jnp.uint32), pltpu.SemaphoreType.DMA((2,))],
    )
    return pl.pallas_call(
        _combine_kernel,
        grid_spec=grid_spec,
        out_shape=jax.ShapeDtypeStruct((n, d), F32),
        compiler_params=pltpu.CompilerParams(
            dimension_semantics=("arbitrary",), vmem_limit_bytes=V7X_VMEM_LIMIT),
        name="combine",
    )(dest, h1, rg, y)


def _rope_tables(seq):
    rows = seq // GRID_W
    inv_freq = jnp.power(ROPE_THETA, -jnp.arange(0, ROPE_AXIS_DIM, 2, dtype=F32) / ROPE_AXIS_DIM)
    ang_r = jnp.arange(rows).astype(F32)[:, None] * inv_freq[None, :]
    ang_c = jnp.arange(GRID_W).astype(F32)[:, None] * inv_freq[None, :]
    shape = (rows, GRID_W, ROPE_AXIS_DIM // 2)
    by_row = lambda a: jnp.broadcast_to(a[:, None, :], shape)
    by_col = lambda a: jnp.broadcast_to(a[None, :, :], shape)
    cr, sr = by_row(jnp.cos(ang_r)), by_row(jnp.sin(ang_r))
    cc, sc = by_col(jnp.cos(ang_c)), by_col(jnp.sin(ang_c))
    cos = jnp.concatenate([cr, cr, cc, cc], axis=-1).reshape(seq, HEAD_DIM)
    sin = jnp.concatenate([-sr, sr, -sc, sc], axis=-1).reshape(seq, HEAD_DIM)
    return cos, sin


def _block_plan(counts, nblk, bm):
    padded = (counts + bm - 1) // bm * bm
    pends = jnp.cumsum(padded)
    blk_start = jnp.arange(nblk, dtype=jnp.int32) * bm
    blk_e = jnp.minimum(jnp.sum(pends[None, :] <= blk_start[:, None], axis=1), N_EXPERTS - 1)
    nvalid = (pends[-1] // bm).reshape(1)
    pad0 = pends - padded + counts
    padn = padded - counts
    return (blk_e.astype(jnp.int32), nvalid.astype(jnp.int32), pad0.astype(jnp.int32),
            padn.astype(jnp.int32))


def kernel(x, meta_tokens, norm1_g, w_in, q_norm_g, k_norm_g, pool_w, pool_scale, w_out, norm2_g,
           w_router_group, b_router_group, w_router_expert, b_router_expert, w_gate, w_up, w_down):
    batch, seq, d = x.shape
    n = batch * seq
    assert norm1_g.shape[0] == 1, "single-layer block"
    assert seq % GRID_W == 0 and seq % 256 == 0

    tm_in = _tile(seq // 2, 512)
    tm_post = _tile(seq, 512)
    tq = _tile(seq, 256)
    bm = 256
    tm_moe = _tile(seq, 256)

    x2d = x.reshape(n, d)
    w_in_bf = w_in[0].astype(BF16)
    cos, sin = _rope_tables(seq)
    qg = q_norm_g[0].reshape(1, HEAD_DIM)
    kg = k_norm_g[0].reshape(1, HEAD_DIM)
    g1 = norm1_g[0].reshape(1, d)

    qt, k, vt, pp = _inproj(x2d, g1, w_in_bf, cos, sin, qg, kg, batch, seq, tm_in)
    ones = jnp.ones((N_META, HEAD_DIM), F32)
    _, km, vmt, pm = _inproj(meta_tokens.astype(F32), g1, w_in_bf, ones, jnp.zeros_like(ones),
                             qg, kg, 1, N_META, N_META)

    attn = _attention(qt, k, vt, km[0], vmt[0, :, 0], tq)

    wr = jnp.concatenate([w_router_expert[0], w_router_group[0],
                          jnp.zeros((d, ROUTER_LANES - N_EXPERTS - N_EXPERT_GROUPS), F32)], axis=1)
    br = jnp.concatenate([b_router_expert[0], b_router_group[0],
                          jnp.zeros((ROUTER_LANES - N_EXPERTS - N_EXPERT_GROUPS,), F32)])
    h1, b_pk, ri, rg, cnt = _post(attn.reshape(n, ATTN_WIDTH), pp, pm, x2d, pool_w[0].astype(BF16),
                             pool_scale[0].reshape(1, POOL_WIDTH), w_out[0].astype(BF16),
                             norm2_g[0].reshape(1, d), wr.astype(BF16),
                             br.reshape(1, ROUTER_LANES), seq, tm_post, 2)

    nblk = -(-(n * TOP_K) // bm) + N_EXPERTS
    dest = _rank(ri, cnt, bm, _tile(seq, 512))[:, :TOP_K].reshape(-1)
    blk_e, nvalid, pad0, padn = _block_plan(cnt[0, :N_EXPERTS].astype(jnp.int32), nblk, bm)
    xs = _dispatch(dest, pad0, padn, nvalid, b_pk, nblk * bm, tm_moe, bm)
    y = _experts(blk_e, nvalid, xs, w_gate[0], w_up[0], w_down[0], bm)
    out = _combine(dest, h1, rg, y, tm_moe)
    return out.reshape(batch, seq, d)
```

```python
import functools
import math

import jax
import jax.numpy as jnp
from jax import lax
from jax.experimental import pallas as pl
from jax.experimental.pallas import tpu as pltpu

N_META = 16
GRID_W = 64
HEAD_DIM = 128
N_Q_HEADS = 8
N_KV_HEADS = 2
Q_PER_KV = N_Q_HEADS // N_KV_HEADS
ATTN_WIDTH = N_Q_HEADS * HEAD_DIM
KV_WIDTH = N_KV_HEADS * HEAD_DIM
POOL_WINDOWS = (2, 4, 8, 16)
N_POOL_GROUPS = len(POOL_WINDOWS)
POOL_GROUP = 256
POOL_WIDTH = N_POOL_GROUPS * POOL_GROUP
ROPE_THETA = 10000.0
ROPE_AXIS_DIM = HEAD_DIM // 2
N_EXPERT_GROUPS = 4
EXPERTS_PER_GROUP = 8
N_EXPERTS = N_EXPERT_GROUPS * EXPERTS_PER_GROUP
TOP_K = 2
EPS = 1e-6

V_ONES_ROWS = 16
V_ROWS = HEAD_DIM + V_ONES_ROWS
F32_SUBLANES = 8
HALO = F32_SUBLANES
ROUTER_LANES = 128
GROUP_LANE0 = N_EXPERTS
DMA_UNROLL = 8
LOG2E = math.log2(math.e)
Q_SCALE = HEAD_DIM ** -0.5 * LOG2E

V7X_VMEM_LIMIT = 56 * 1024 * 1024

F32 = jnp.float32
BF16 = jnp.bfloat16


def _tile(n, pref):
    t = min(n, pref)
    while n % t:
        t //= 2
    return t


def _inproj_kernel(x_ref, g1_ref, w_ref, cos_ref, sin_ref, qg_ref, kg_ref,
                   qt_ref, k_ref, vt_ref, p_ref):
    tm = x_ref.shape[0]
    xf = x_ref[...]
    ms = jnp.mean(xf * xf, axis=-1, keepdims=True)
    a = (xf * lax.rsqrt(ms + EPS) * g1_ref[...]).astype(BF16)
    cos = cos_ref[...]
    sin = sin_ref[...]
    lane = lax.broadcasted_iota(jnp.int32, (tm, HEAD_DIM), 1)
    first_half = (lane % ROPE_AXIS_DIM) < (ROPE_AXIS_DIM // 2)

    def norm_rope(hd, g, scale):
        n = hd * lax.rsqrt(jnp.mean(hd * hd, axis=-1, keepdims=True) + EPS) * g
        partner = jnp.where(first_half,
                            pltpu.roll(n, HEAD_DIM - ROPE_AXIS_DIM // 2, 1),
                            pltpu.roll(n, ROPE_AXIS_DIM // 2, 1))
        return (n * cos + partner * sin) * scale

    qg = qg_ref[...]
    kg = kg_ref[...]
    for c in range(N_Q_HEADS // 2):
        pr = jnp.dot(a, w_ref[:, c * 256:(c + 1) * 256], preferred_element_type=F32)
        for j in range(2):
            qh = norm_rope(pr[:, j * HEAD_DIM:(j + 1) * HEAD_DIM], qg, Q_SCALE)
            qt_ref[0, 2 * c + j] = qh.T.astype(BF16)
    pr = jnp.dot(a, w_ref[:, ATTN_WIDTH:ATTN_WIDTH + KV_WIDTH], preferred_element_type=F32)
    for j in range(N_KV_HEADS):
        k_ref[0, j] = norm_rope(pr[:, j * HEAD_DIM:(j + 1) * HEAD_DIM], kg, 1.0).astype(BF16)
    pr = jnp.dot(a, w_ref[:, ATTN_WIDTH + KV_WIDTH:ATTN_WIDTH + 2 * KV_WIDTH],
                 preferred_element_type=F32)
    for j in range(N_KV_HEADS):
        vt_ref[0, j, 0, 0:HEAD_DIM, :] = pr[:, j * HEAD_DIM:(j + 1) * HEAD_DIM].T.astype(BF16)
        vt_ref[0, j, 0, HEAD_DIM:V_ROWS, :] = jnp.ones((V_ONES_ROWS, tm), BF16)
    p_ref[...] = jnp.dot(a, w_ref[:, ATTN_WIDTH + 2 * KV_WIDTH:], preferred_element_type=F32)


def _inproj(x2d, g1, w_bf, cos, sin, qg, kg, batch, seq, tm):
    n, d = x2d.shape
    nj = seq // tm
    in_w = w_bf.shape[1]
    const = lambda i: (0, 0)
    return pl.pallas_call(
        _inproj_kernel,
        grid=(n // tm,),
        in_specs=[
            pl.BlockSpec((tm, d), lambda i: (i, 0)),
            pl.BlockSpec((1, d), const),
            pl.BlockSpec((d, in_w), const),
            pl.BlockSpec((tm, HEAD_DIM), lambda i: (i % nj, 0)),
            pl.BlockSpec((tm, HEAD_DIM), lambda i: (i % nj, 0)),
            pl.BlockSpec((1, HEAD_DIM), const),
            pl.BlockSpec((1, HEAD_DIM), const),
        ],
        out_specs=[
            pl.BlockSpec((1, N_Q_HEADS, HEAD_DIM, tm), lambda i: (i // nj, 0, 0, i % nj)),
            pl.BlockSpec((1, N_KV_HEADS, tm, HEAD_DIM), lambda i: (i // nj, 0, i % nj, 0)),
            pl.BlockSpec((1, N_KV_HEADS, 1, V_ROWS, tm), lambda i: (i // nj, 0, i % nj, 0, 0)),
            pl.BlockSpec((tm, POOL_WIDTH), lambda i: (i, 0)),
        ],
        out_shape=[
            jax.ShapeDtypeStruct((batch, N_Q_HEADS, HEAD_DIM, seq), BF16),
            jax.ShapeDtypeStruct((batch, N_KV_HEADS, seq, HEAD_DIM), BF16),
            jax.ShapeDtypeStruct((batch, N_KV_HEADS, nj, V_ROWS, tm), BF16),
            jax.ShapeDtypeStruct((n, POOL_WIDTH), F32),
        ],
        compiler_params=pltpu.CompilerParams(
            dimension_semantics=("arbitrary",), vmem_limit_bytes=V7X_VMEM_LIMIT),
        name="inproj",
    )(x2d, g1, w_bf, cos, sin, qg, kg)


def _attn_kernel(qt_ref, k_ref, vt_ref, km_ref, vmt_ref, o_ref, s_scr, acc_scr):
    tq = qt_ref.shape[3]
    nk = vt_ref.shape[2]
    tk = vt_ref.shape[4]
    unroll = next(u for u in (8, 4, 2) if nk % u == 0)
    qt = jnp.concatenate([qt_ref[0, g] for g in range(Q_PER_KV)], axis=1)

    def scores(j):
        off = pl.multiple_of(j * tk, tk)
        return jnp.dot(k_ref[0, 0, pl.ds(off, tk), :], qt, preferred_element_type=F32)

    s0 = jnp.dot(km_ref[0], qt, preferred_element_type=F32)
    m0 = jnp.max(s0, axis=0, keepdims=True)
    p0 = jnp.exp2(s0 - m0).astype(BF16)
    acc_scr[...] = jnp.dot(vmt_ref[0], p0, preferred_element_type=F32)
    s = scores(0)
    s_scr[0] = s
    mt0 = jnp.max(s, axis=0, keepdims=True)

    def step(j, cur, m, mt):
        s_next = scores(jnp.minimum(j + 1, nk - 1))
        s_scr[1 - cur] = s_next
        mt_next = jnp.max(s_next, axis=0, keepdims=True)
        m_new = jnp.maximum(m, mt)
        alpha = jnp.exp2(m - m_new)
        p = jnp.exp2(s_scr[cur] - m_new).astype(BF16)
        pv = jnp.dot(vt_ref[0, 0, j], p, preferred_element_type=F32)
        acc_scr[...] = alpha * acc_scr[...] + pv
        return m_new, mt_next

    def body(jb, carry):
        m, mt = carry
        for u in range(unroll):
            m, mt = step(unroll * jb + u, u % 2, m, mt)
        return m, mt

    lax.fori_loop(0, nk // unroll, body, (m0, mt0))
    out_t = acc_scr[0:HEAD_DIM, :] / acc_scr[HEAD_DIM:HEAD_DIM + 1, :]
    for g in range(Q_PER_KV):
        o_ref[0, :, g * HEAD_DIM:(g + 1) * HEAD_DIM] = out_t[:, g * tq:(g + 1) * tq].T.astype(BF16)


def _attention(qt, k, vt, km, vmt, tq):
    batch, _, _, seq = qt.shape
    nk, tk = vt.shape[2], vt.shape[4]
    assert nk % 2 == 0
    gw = Q_PER_KV * HEAD_DIM
    return pl.pallas_call(
        _attn_kernel,
        grid=(batch, N_KV_HEADS, seq // tq),
        in_specs=[
            pl.BlockSpec((1, Q_PER_KV, HEAD_DIM, tq), lambda b, h, i: (b, h, 0, i)),
            pl.BlockSpec((1, 1, seq, HEAD_DIM), lambda b, h, i: (b, h, 0, 0)),
            pl.BlockSpec((1, 1, nk, V_ROWS, tk), lambda b, h, i: (b, h, 0, 0, 0)),
            pl.BlockSpec((1, N_META, HEAD_DIM), lambda b, h, i: (h, 0, 0)),
            pl.BlockSpec((1, V_ROWS, N_META), lambda b, h, i: (h, 0, 0)),
        ],
        out_specs=pl.BlockSpec((1, tq, gw), lambda b, h, i: (b, i, h)),
        out_shape=jax.ShapeDtypeStruct((batch, seq, ATTN_WIDTH), BF16),
        scratch_shapes=[pltpu.VMEM((2, tk, Q_PER_KV * tq), F32),
                        pltpu.VMEM((V_ROWS, Q_PER_KV * tq), F32)],
        compiler_params=pltpu.CompilerParams(
            dimension_semantics=("arbitrary", "arbitrary", "arbitrary"),
            vmem_limit_bytes=V7X_VMEM_LIMIT),
        name="attention",
    )(qt, k, vt, km, vmt)


def _post_kernel(attn_ref, pp_ref, prev_ref, next_ref, pm_ref, x_ref, pw_ref, ps_ref, wo_ref,
                 g2_ref, wr_ref, br_ref, h1_ref, b_ref, ri_ref, rg_ref, cnt_ref, ext_ref, mix_ref,
                 *, nj, seq, nsub):
    tm = x_ref.shape[0]
    ts = tm // nsub
    j = pl.program_id(0) % nj
    ext_ref[0:HALO] = jnp.where(j == 0, pm_ref[N_META - HALO:N_META, :], prev_ref[...])
    ext_ref[HALO:HALO + tm] = pp_ref[...]
    ext_ref[HALO + tm:2 * HALO + tm] = jnp.where(j == nj - 1, 0.0, next_ref[...])

    @pl.when(pl.program_id(0) == 0)
    def _():
        cnt_ref[...] = jnp.zeros_like(cnt_ref)

    lane = lax.broadcasted_iota(jnp.int32, (ts, ROUTER_LANES), 1)
    lane_f = lane.astype(F32)
    neg = jnp.float32(-jnp.inf)
    big = jnp.float32(ROUTER_LANES)
    is_group = (lane >= GROUP_LANE0) & (lane < GROUP_LANE0 + N_EXPERT_GROUPS)

    for s in range(nsub):
        r0 = s * ts
        rows = slice(r0, r0 + ts)
        mix_ref[rows, 0:ATTN_WIDTH] = attn_ref[rows, :]
        for g, win in enumerate(POOL_WINDOWS):
            c0, c1 = g * POOL_GROUP, (g + 1) * POOL_GROUP
            half = win // 2
            ext_rows = ts + 2 * HALO
            e = ext_ref[r0:r0 + ext_rows, c0:c1]
            ahead = lambda a, k: pltpu.roll(a, ext_rows - k, 0)
            behind = lambda a, k: pltpu.roll(a, k, 0)
            fwd = e
            k = 1
            while k < half:
                fwd = fwd + ahead(fwd, k)
                k *= 2
            tot = (fwd + behind(fwd, half))[HALO:HALO + ts]
            if s == nsub - 1:
                t8 = j * tm + (tm - HALO) + lax.broadcasted_iota(jnp.int32, (HALO, 1), 0)
                cnt8 = (jnp.minimum(t8 + half, seq) - (t8 - half)).astype(F32)
                mean = jnp.concatenate([tot[0:ts - HALO] * (1.0 / win), tot[ts - HALO:ts] / cnt8], axis=0)
            else:
                mean = tot * (1.0 / win)
            mg = (mean - pp_ref[rows, c0:c1]).astype(BF16)
            yg = jnp.dot(mg, pw_ref[g], preferred_element_type=F32) * ps_ref[:, c0:c1]
            mix_ref[rows, ATTN_WIDTH + c0:ATTN_WIDTH + c1] = yg.astype(BF16)

        h1 = x_ref[rows, :] + jnp.dot(mix_ref[rows, :], wo_ref[...], preferred_element_type=F32)
        h1_ref[rows, :] = h1
        ms = jnp.mean(h1 * h1, axis=-1, keepdims=True)
        bf = h1 * lax.rsqrt(ms + EPS) * g2_ref[...]
        b_ref[rows, :] = bf
        logits = jnp.dot(bf.astype(BF16), wr_ref[...], preferred_element_type=F32) + br_ref[...]

        gl = jnp.where(is_group, logits, neg)
        gmax = jnp.max(gl, axis=1, keepdims=True)
        gidx = jnp.min(jnp.where(gl == gmax, lane_f, big), axis=1, keepdims=True) - GROUP_LANE0
        g_p = 1.0 / jnp.sum(jnp.exp(gl - gmax), axis=1, keepdims=True)
        in_group = (lane // EXPERTS_PER_GROUP).astype(F32) == gidx
        el = jnp.where(in_group & (lane < N_EXPERTS), logits, neg)
        m1 = jnp.max(el, axis=1, keepdims=True)
        i1 = jnp.min(jnp.where(el == m1, lane_f, big), axis=1, keepdims=True)
        el2 = jnp.where(lane_f == i1, neg, el)
        m2 = jnp.max(el2, axis=1, keepdims=True)
        i2 = jnp.min(jnp.where(el2 == m2, lane_f, big), axis=1, keepdims=True)
        p2 = jnp.exp(m2 - m1)
        den = 1.0 + p2
        gate1 = g_p / den
        gate2 = g_p * p2 / den
        ri_ref[rows, :] = jnp.where(lane == 0, i1, jnp.where(lane == 1, i2, 0.0)).astype(jnp.int32)
        rg_ref[rows, :] = jnp.where(lane == 0, gate1, jnp.where(lane == 1, gate2, 0.0))

        chosen = jnp.where(lane_f == i1, 1.0, 0.0) + jnp.where(lane_f == i2, 1.0, 0.0)
        cnt_ref[...] += jnp.broadcast_to(jnp.sum(chosen, axis=0, keepdims=True), cnt_ref.shape)


def _post(attn2d, pp, pm, x2d, pw_bf, ps, wo_bf, g2, wr_bf, br, seq, tm, nsub):
    n, d = x2d.shape
    nj = seq // tm
    hb = tm // HALO
    last_hb = n // HALO - 1
    const2 = lambda i: (0, 0)
    row = lambda i: (i, 0)
    return pl.pallas_call(
        functools.partial(_post_kernel, nj=nj, seq=seq, nsub=nsub),
        grid=(n // tm,),
        in_specs=[
            pl.BlockSpec((tm, ATTN_WIDTH), row),
            pl.BlockSpec((tm, POOL_WIDTH), row),
            pl.BlockSpec((HALO, POOL_WIDTH), lambda i: (jnp.maximum(i * hb - 1, 0), 0)),
            pl.BlockSpec((HALO, POOL_WIDTH), lambda i: (jnp.minimum((i + 1) * hb, last_hb), 0)),
            pl.BlockSpec((N_META, POOL_WIDTH), const2),
            pl.BlockSpec((tm, d), row),
            pl.BlockSpec((N_POOL_GROUPS, POOL_GROUP, POOL_GROUP), lambda i: (0, 0, 0)),
            pl.BlockSpec((1, POOL_WIDTH), const2),
            pl.BlockSpec((ATTN_WIDTH + POOL_WIDTH, d), const2),
            pl.BlockSpec((1, d), const2),
            pl.BlockSpec((d, ROUTER_LANES), const2),
            pl.BlockSpec((1, ROUTER_LANES), const2),
        ],
        out_specs=[
            pl.BlockSpec((tm, d), row),
            pl.BlockSpec((tm, d), row),
            pl.BlockSpec((tm, ROUTER_LANES), row),
            pl.BlockSpec((tm, ROUTER_LANES), row),
            pl.BlockSpec((8, ROUTER_LANES), const2),
        ],
        out_shape=[
            jax.ShapeDtypeStruct((n, d), F32),
            jax.ShapeDtypeStruct((n, d), F32),
            jax.ShapeDtypeStruct((n, ROUTER_LANES), jnp.int32),
            jax.ShapeDtypeStruct((n, ROUTER_LANES), F32),
            jax.ShapeDtypeStruct((8, ROUTER_LANES), F32),
        ],
        scratch_shapes=[pltpu.VMEM((tm + 2 * HALO, POOL_WIDTH), F32),
                        pltpu.VMEM((tm, ATTN_WIDTH + POOL_WIDTH), BF16)],
        compiler_params=pltpu.CompilerParams(
            dimension_semantics=("arbitrary",), vmem_limit_bytes=V7X_VMEM_LIMIT),
        name="post",
    )(attn2d, pp, pp, pp, pm, x2d, pw_bf, ps, wo_bf, g2, wr_bf, br)


def _rank_kernel(ri_ref, cnt_ref, lower_ref, dest_ref, carry, pstart, *, bm):
    t = ri_ref.shape[0]

    @pl.when(pl.program_id(0) == 0)
    def _():
        counts = cnt_ref[...]
        padded = jnp.floor((counts + (bm - 1)) * (1.0 / bm)) * bm
        lane8 = lax.broadcasted_iota(jnp.int32, (8, ROUTER_LANES), 1)
        incl = padded
        shift = 1
        while shift < ROUTER_LANES:
            incl = incl + jnp.where(lane8 >= shift, pltpu.roll(incl, shift, 1), 0.0)
            shift *= 2
        pstart[...] = (incl - padded)[0:1]
        carry[...] = jnp.zeros_like(carry)

    lane = lax.broadcasted_iota(jnp.int32, (t, ROUTER_LANES), 1)
    ri = ri_ref[...]
    oh0 = lane == ri[:, 0:1]
    oh1 = lane == ri[:, 1:2]
    ohf = jnp.where(oh0, 1.0, 0.0) + jnp.where(oh1, 1.0, 0.0)
    prefix = jnp.dot(lower_ref[...], ohf.astype(BF16), preferred_element_type=F32)
    base = prefix + carry[...] + pstart[...]
    d0 = jnp.sum(jnp.where(oh0, base, 0.0), axis=1, keepdims=True)
    d1 = jnp.sum(jnp.where(oh1, base, 0.0), axis=1, keepdims=True)
    dest_ref[...] = jnp.where(lane == 0, d0, jnp.where(lane == 1, d1, 0.0)).astype(jnp.int32)
    carry[...] += jnp.sum(ohf, axis=0, keepdims=True)


def _rank(ri, cnt, bm, t):
    n = ri.shape[0]
    lower = jnp.tril(jnp.ones((t, t), BF16), -1)
    return pl.pallas_call(
        functools.partial(_rank_kernel, bm=bm),
        grid=(n // t,),
        in_specs=[
            pl.BlockSpec((t, ROUTER_LANES), lambda i: (i, 0)),
            pl.BlockSpec((8, ROUTER_LANES), lambda i: (0, 0)),
            pl.BlockSpec((t, t), lambda i: (0, 0)),
        ],
        out_specs=pl.BlockSpec((t, ROUTER_LANES), lambda i: (i, 0)),
        out_shape=jax.ShapeDtypeStruct((n, ROUTER_LANES), jnp.int32),
        scratch_shapes=[pltpu.VMEM((1, ROUTER_LANES), F32), pltpu.VMEM((1, ROUTER_LANES), F32)],
        compiler_params=pltpu.CompilerParams(
            dimension_semantics=("arbitrary",), vmem_limit_bytes=V7X_VMEM_LIMIT),
        name="rank",
    )(ri, cnt, lower)


def _dispatch_kernel(dest_ref, pend_ref, padn_ref, nvalid_ref, b_ref, xs_ref, zeros, sem, zsem, *, bm):
    tm = b_ref.shape[0]
    i = pl.program_id(0)

    @pl.when(i == 0)
    def _():
        zeros[...] = jnp.zeros_like(zeros)
        nblk = xs_ref.shape[0] // bm

        def pad_copy(e, g):
            start = pl.multiple_of(pend_ref[e] - (g + 1) * F32_SUBLANES, F32_SUBLANES)
            return pltpu.make_async_copy(zeros.at[pl.ds(0, F32_SUBLANES)], xs_ref.at[pl.ds(start, F32_SUBLANES)], zsem)

        def tail_copy(blk):
            return pltpu.make_async_copy(zeros, xs_ref.at[pl.ds(pl.multiple_of(blk * bm, bm), bm)], zsem)

        def n_groups(e):
            return (padn_ref[e] + (F32_SUBLANES - 1)) // F32_SUBLANES

        def per_expert(e, c):
            lax.fori_loop(0, n_groups(e), lambda g, c2: (pad_copy(e, g).start(), c2)[1], 0)
            return c
        lax.fori_loop(0, N_EXPERTS, per_expert, 0)
        lax.fori_loop(nvalid_ref[0], nblk, lambda blk, c: (tail_copy(blk).start(), c)[1], 0)

        def per_expert_wait(e, c):
            lax.fori_loop(0, n_groups(e), lambda g, c2: (pad_copy(e, g).wait(), c2)[1], 0)
            return c
        lax.fori_loop(0, N_EXPERTS, per_expert_wait, 0)
        lax.fori_loop(nvalid_ref[0], nblk, lambda blk, c: (tail_copy(blk).wait(), c)[1], 0)

    def row_copy(r, k):
        p = dest_ref[(i * tm + r) * TOP_K + k]
        return pltpu.make_async_copy(b_ref.at[pl.ds(r, 1)], xs_ref.at[pl.ds(p, 1)], sem)

    def issue(r8, c):
        for u in range(DMA_UNROLL):
            for k in range(TOP_K):
                row_copy(r8 * DMA_UNROLL + u, k).start()
        return c
    lax.fori_loop(0, tm // DMA_UNROLL, issue, 0)

    def drain(r8, c):
        for u in range(DMA_UNROLL):
            for k in range(TOP_K):
                row_copy(r8 * DMA_UNROLL + u, k).wait()
        return c
    lax.fori_loop(0, tm // DMA_UNROLL, drain, 0)


def _dispatch(dest, pend, padn, nvalid, b_pk, n_slots, tm, bm):
    n, hw = b_pk.shape
    grid_spec = pltpu.PrefetchScalarGridSpec(
        num_scalar_prefetch=4,
        grid=(n // tm,),
        in_specs=[pl.BlockSpec((tm, hw), lambda i, *_: (i, 0))],
        out_specs=pl.BlockSpec(memory_space=pl.ANY),
        scratch_shapes=[pltpu.VMEM((bm, hw), F32), pltpu.SemaphoreType.DMA(()),
                        pltpu.SemaphoreType.DMA(())],
    )
    return pl.pallas_call(
        functools.partial(_dispatch_kernel, bm=bm),
        grid_spec=grid_spec,
        out_shape=jax.ShapeDtypeStruct((n_slots, hw), F32),
        compiler_params=pltpu.CompilerParams(
            dimension_semantics=("arbitrary",), vmem_limit_bytes=V7X_VMEM_LIMIT,
            has_side_effects=True),
        name="dispatch",
    )(dest, pend, padn, nvalid, b_pk)


def _expert_kernel(blk_e_ref, nvalid_ref, xs_ref, wg_ref, wu_ref, wd_ref, y_ref,
                   wg_bf, wu_bf, wd_bf):
    i = pl.program_id(0)
    nvalid = nvalid_ref[0]
    new_expert = jnp.logical_or(i == 0, blk_e_ref[i] != blk_e_ref[jnp.maximum(i - 1, 0)])

    @pl.when(jnp.logical_and(new_expert, i < nvalid))
    def _():
        wg_bf[...] = wg_ref[0].astype(BF16)
        wu_bf[...] = wu_ref[0].astype(BF16)
        wd_bf[...] = wd_ref[0].astype(BF16)

    @pl.when(i < nvalid)
    def _():
        xb = xs_ref[...].astype(BF16)
        hg = jnp.dot(xb, wg_bf[...], preferred_element_type=F32)
        hu = jnp.dot(xb, wu_bf[...], preferred_element_type=F32)
        h = (hg * jax.nn.sigmoid(hg) * hu).astype(BF16)
        y_ref[...] = jnp.dot(h, wd_bf[...], preferred_element_type=F32)

    @pl.when(i >= nvalid)
    def _():
        y_ref[...] = jnp.zeros_like(y_ref)


def _experts(blk_e, nvalid, xs, w_gate, w_up, w_down, bm):
    n_slots, hw = xs.shape
    nblk = n_slots // bm
    _, d, de = w_gate.shape
    last = lambda i, nv: jnp.maximum(jnp.minimum(i, nv[0] - 1), 0)
    grid_spec = pltpu.PrefetchScalarGridSpec(
        num_scalar_prefetch=2,
        grid=(nblk,),
        in_specs=[
            pl.BlockSpec((bm, hw), lambda i, be, nv: (last(i, nv), 0)),
            pl.BlockSpec((1, d, de), lambda i, be, nv: (be[last(i, nv)], 0, 0)),
            pl.BlockSpec((1, d, de), lambda i, be, nv: (be[last(i, nv)], 0, 0)),
            pl.BlockSpec((1, de, d), lambda i, be, nv: (be[last(i, nv)], 0, 0)),
        ],
        out_specs=pl.BlockSpec((bm, hw), lambda i, be, nv: (i, 0)),
        scratch_shapes=[pltpu.VMEM((d, de), BF16), pltpu.VMEM((d, de), BF16),
                        pltpu.VMEM((de, d), BF16)],
    )
    return pl.pallas_call(
        _expert_kernel,
        grid_spec=grid_spec,
        out_shape=jax.ShapeDtypeStruct((n_slots, hw), F32),
        compiler_params=pltpu.CompilerParams(
            dimension_semantics=("arbitrary",), vmem_limit_bytes=V7X_VMEM_LIMIT),
        name="experts",
    )(blk_e, nvalid, xs, w_gate, w_up, w_down)


def _combine_kernel(dest_ref, h1_ref, rg_ref, y_hbm, out_ref, ybuf, sem):
    tm = h1_ref.shape[0]
    i = pl.program_id(0)
    n_steps = pl.num_programs(0)
    slot = i % 2

    def row_copy(tile, r, k, s):
        p = dest_ref[(tile * tm + r) * TOP_K + k]
        return pltpu.make_async_copy(y_hbm.at[pl.ds(p, 1)], ybuf.at[s, k, pl.ds(r, 1)], sem.at[s])

    def issue(tile, s):
        def body(r8, c):
            for u in range(DMA_UNROLL):
                for k in range(TOP_K):
                    row_copy(tile, r8 * DMA_UNROLL + u, k, s).start()
            return c
        lax.fori_loop(0, tm // DMA_UNROLL, body, 0)

    @pl.when(i == 0)
    def _():
        issue(0, 0)

    @pl.when(i + 1 < n_steps)
    def _():
        issue(i + 1, 1 - slot)

    def drain(r8, c):
        for u in range(DMA_UNROLL):
            for k in range(TOP_K):
                row_copy(i, r8 * DMA_UNROLL + u, k, slot).wait()
        return c
    lax.fori_loop(0, tm // DMA_UNROLL, drain, 0)

    rg = rg_ref[...]
    out = h1_ref[...]
    for k in range(TOP_K):
        out = out + rg[:, k:k + 1] * ybuf[slot, k]
    out_ref[...] = out


def _combine(dest, h1, rg, y, tm):
    n, d = h1.shape
    hw = y.shape[1]
    row = lambda i, de: (i, 0)
    grid_spec = pltpu.PrefetchScalarGridSpec(
        num_scalar_prefetch=1,
        grid=(n // tm,),
        in_specs=[
            pl.BlockSpec((tm, d), row),
            pl.BlockSpec((tm, ROUTER_LANES), row),
            pl.BlockSpec(memory_space=pl.ANY),
        ],
        out_specs=pl.BlockSpec((tm, d), row),
        scratch_shapes=[pltpu.VMEM((2, TOP_K, tm, hw), F32), pltpu.SemaphoreType.DMA((2,))],
    )
    return pl.pallas_call(
        _combine_kernel,
        grid_spec=grid_spec,
        out_shape=jax.ShapeDtypeStruct((n, d), F32),
        compiler_params=pltpu.CompilerParams(
            dimension_semantics=("arbitrary",), vmem_limit_bytes=V7X_VMEM_LIMIT),
        name="combine",
    )(dest, h1, rg, y)


def _rope_tables(seq):
    rows = seq // GRID_W
    inv_freq = jnp.power(ROPE_THETA, -jnp.arange(0, ROPE_AXIS_DIM, 2, dtype=F32) / ROPE_AXIS_DIM)
    ang_r = jnp.arange(rows).astype(F32)[:, None] * inv_freq[None, :]
    ang_c = jnp.arange(GRID_W).astype(F32)[:, None] * inv_freq[None, :]
    shape = (rows, GRID_W, ROPE_AXIS_DIM // 2)
    by_row = lambda a: jnp.broadcast_to(a[:, None, :], shape)
    by_col = lambda a: jnp.broadcast_to(a[None, :, :], shape)
    cr, sr = by_row(jnp.cos(ang_r)), by_row(jnp.sin(ang_r))
    cc, sc = by_col(jnp.cos(ang_c)), by_col(jnp.sin(ang_c))
    cos = jnp.concatenate([cr, cr, cc, cc], axis=-1).reshape(seq, HEAD_DIM)
    sin = jnp.concatenate([-sr, sr, -sc, sc], axis=-1).reshape(seq, HEAD_DIM)
    return cos, sin


def _block_plan(counts, nblk, bm):
    padded = (counts + bm - 1) // bm * bm
    pends = jnp.cumsum(padded)
    blk_start = jnp.arange(nblk, dtype=jnp.int32) * bm
    blk_e = jnp.minimum(jnp.sum(pends[None, :] <= blk_start[:, None], axis=1), N_EXPERTS - 1)
    nvalid = (pends[-1] // bm).reshape(1)
    padn = padded - counts
    return (blk_e.astype(jnp.int32), nvalid.astype(jnp.int32), pends.astype(jnp.int32),
            padn.astype(jnp.int32))


def kernel(x, meta_tokens, norm1_g, w_in, q_norm_g, k_norm_g, pool_w, pool_scale, w_out, norm2_g,
           w_router_group, b_router_group, w_router_expert, b_router_expert, w_gate, w_up, w_down):
    batch, seq, d = x.shape
    n = batch * seq
    assert norm1_g.shape[0] == 1, "single-layer block"
    assert seq % GRID_W == 0 and seq % 256 == 0

    tm_in = _tile(seq // 2, 512)
    tm_post = _tile(seq, 512)
    tq = _tile(seq, 256)
    bm = 256
    tm_moe = _tile(seq, 256)

    x2d = x.reshape(n, d)
    w_in_bf = w_in[0].astype(BF16)
    cos, sin = _rope_tables(seq)
    qg = q_norm_g[0].reshape(1, HEAD_DIM)
    kg = k_norm_g[0].reshape(1, HEAD_DIM)
    g1 = norm1_g[0].reshape(1, d)

    qt, k, vt, pp = _inproj(x2d, g1, w_in_bf, cos, sin, qg, kg, batch, seq, tm_in)
    ones = jnp.ones((N_META, HEAD_DIM), F32)
    _, km, vmt, pm = _inproj(meta_tokens.astype(F32), g1, w_in_bf, ones, jnp.zeros_like(ones),
                             qg, kg, 1, N_META, N_META)

    attn = _attention(qt, k, vt, km[0], vmt[0, :, 0], tq)

    wr = jnp.concatenate([w_router_expert[0], w_router_group[0],
                          jnp.zeros((d, ROUTER_LANES - N_EXPERTS - N_EXPERT_GROUPS), F32)], axis=1)
    br = jnp.concatenate([b_router_expert[0], b_router_group[0],
                          jnp.zeros((ROUTER_LANES - N_EXPERTS - N_EXPERT_GROUPS,), F32)])
    h1, b_pk, ri, rg, cnt = _post(attn.reshape(n, ATTN_WIDTH), pp, pm, x2d, pool_w[0].astype(BF16),
                             pool_scale[0].reshape(1, POOL_WIDTH), w_out[0].astype(BF16),
                             norm2_g[0].reshape(1, d), wr.astype(BF16),
                             br.reshape(1, ROUTER_LANES), seq, tm_post, 2)

    nblk = -(-(n * TOP_K) // bm) + N_EXPERTS
    dest = _rank(ri, cnt, bm, _tile(seq, 512))[:, :TOP_K].reshape(-1)
    blk_e, nvalid, pends, padn = _block_plan(cnt[0, :N_EXPERTS].astype(jnp.int32), nblk, bm)
    xs = _dispatch(dest, pends, padn, nvalid, b_pk, nblk * bm, tm_moe, bm)
    y = _experts(blk_e, nvalid, xs, w_gate[0], w_up[0], w_down[0], bm)
    out = _combine(dest, h1, rg, y, tm_moe)
    return out.reshape(batch, seq, d)
```

```python
import functools
import math

import jax
import jax.numpy as jnp
from jax import lax
from jax.experimental import pallas as pl
from jax.experimental.pallas import tpu as pltpu

N_META = 16
GRID_W = 64
HEAD_DIM = 128
N_Q_HEADS = 8
N_KV_HEADS = 2
Q_PER_KV = N_Q_HEADS // N_KV_HEADS
ATTN_WIDTH = N_Q_HEADS * HEAD_DIM
KV_WIDTH = N_KV_HEADS * HEAD_DIM
POOL_WINDOWS = (2, 4, 8, 16)
N_POOL_GROUPS = len(POOL_WINDOWS)
POOL_GROUP = 256
POOL_WIDTH = N_POOL_GROUPS * POOL_GROUP
ROPE_THETA = 10000.0
ROPE_AXIS_DIM = HEAD_DIM // 2
N_EXPERT_GROUPS = 4
EXPERTS_PER_GROUP = 8
N_EXPERTS = N_EXPERT_GROUPS * EXPERTS_PER_GROUP
TOP_K = 2
EPS = 1e-6

V_ONES_ROWS = 16
V_ROWS = HEAD_DIM + V_ONES_ROWS
F32_SUBLANES = 8
HALO = F32_SUBLANES
ROUTER_LANES = 128
GROUP_LANE0 = N_EXPERTS
DMA_UNROLL = 16
LOG2E = math.log2(math.e)
Q_SCALE = HEAD_DIM ** -0.5 * LOG2E

V7X_VMEM_LIMIT = 56 * 1024 * 1024

F32 = jnp.float32
BF16 = jnp.bfloat16


def _tile(n, pref):
    t = min(n, pref)
    while n % t:
        t //= 2
    return t


def _inproj_kernel(x_ref, g1_ref, w_ref, cos_ref, sin_ref, qg_ref, kg_ref,
                   qt_ref, k_ref, vt_ref, p_ref):
    tm = x_ref.shape[0]
    xf = x_ref[...]
    ms = jnp.mean(xf * xf, axis=-1, keepdims=True)
    a = (xf * lax.rsqrt(ms + EPS) * g1_ref[...]).astype(BF16)
    cos = cos_ref[...]
    sin = sin_ref[...]
    lane = lax.broadcasted_iota(jnp.int32, (tm, HEAD_DIM), 1)
    first_half = (lane % ROPE_AXIS_DIM) < (ROPE_AXIS_DIM // 2)

    def norm_rope(hd, g, scale):
        n = hd * lax.rsqrt(jnp.mean(hd * hd, axis=-1, keepdims=True) + EPS) * g
        partner = jnp.where(first_half,
                            pltpu.roll(n, HEAD_DIM - ROPE_AXIS_DIM // 2, 1),
                            pltpu.roll(n, ROPE_AXIS_DIM // 2, 1))
        return (n * cos + partner * sin) * scale

    qg = qg_ref[...]
    kg = kg_ref[...]
    for c in range(N_Q_HEADS // 2):
        pr = jnp.dot(a, w_ref[:, c * 256:(c + 1) * 256], preferred_element_type=F32)
        for j in range(2):
            qh = norm_rope(pr[:, j * HEAD_DIM:(j + 1) * HEAD_DIM], qg, Q_SCALE)
            qt_ref[0, 2 * c + j] = qh.T.astype(BF16)
    pr = jnp.dot(a, w_ref[:, ATTN_WIDTH:ATTN_WIDTH + KV_WIDTH], preferred_element_type=F32)
    for j in range(N_KV_HEADS):
        k_ref[0, j] = norm_rope(pr[:, j * HEAD_DIM:(j + 1) * HEAD_DIM], kg, 1.0).astype(BF16)
    pr = jnp.dot(a, w_ref[:, ATTN_WIDTH + KV_WIDTH:ATTN_WIDTH + 2 * KV_WIDTH],
                 preferred_element_type=F32)
    for j in range(N_KV_HEADS):
        vt_ref[0, j, 0, 0:HEAD_DIM, :] = pr[:, j * HEAD_DIM:(j + 1) * HEAD_DIM].T.astype(BF16)
        vt_ref[0, j, 0, HEAD_DIM:V_ROWS, :] = jnp.ones((V_ONES_ROWS, tm), BF16)
    p_ref[...] = jnp.dot(a, w_ref[:, ATTN_WIDTH + 2 * KV_WIDTH:], preferred_element_type=F32)


def _inproj(x2d, g1, w_bf, cos, sin, qg, kg, batch, seq, tm):
    n, d = x2d.shape
    nj = seq // tm
    in_w = w_bf.shape[1]
    const = lambda i: (0, 0)
    return pl.pallas_call(
        _inproj_kernel,
        grid=(n // tm,),
        in_specs=[
            pl.BlockSpec((tm, d), lambda i: (i, 0)),
            pl.BlockSpec((1, d), const),
            pl.BlockSpec((d, in_w), const),
            pl.BlockSpec((tm, HEAD_DIM), lambda i: (i % nj, 0)),
            pl.BlockSpec((tm, HEAD_DIM), lambda i: (i % nj, 0)),
            pl.BlockSpec((1, HEAD_DIM), const),
            pl.BlockSpec((1, HEAD_DIM), const),
        ],
        out_specs=[
            pl.BlockSpec((1, N_Q_HEADS, HEAD_DIM, tm), lambda i: (i // nj, 0, 0, i % nj)),
            pl.BlockSpec((1, N_KV_HEADS, tm, HEAD_DIM), lambda i: (i // nj, 0, i % nj, 0)),
            pl.BlockSpec((1, N_KV_HEADS, 1, V_ROWS, tm), lambda i: (i // nj, 0, i % nj, 0, 0)),
            pl.BlockSpec((tm, POOL_WIDTH), lambda i: (i, 0)),
        ],
        out_shape=[
            jax.ShapeDtypeStruct((batch, N_Q_HEADS, HEAD_DIM, seq), BF16),
            jax.ShapeDtypeStruct((batch, N_KV_HEADS, seq, HEAD_DIM), BF16),
            jax.ShapeDtypeStruct((batch, N_KV_HEADS, nj, V_ROWS, tm), BF16),
            jax.ShapeDtypeStruct((n, POOL_WIDTH), F32),
        ],
        compiler_params=pltpu.CompilerParams(
            dimension_semantics=("arbitrary",), vmem_limit_bytes=V7X_VMEM_LIMIT),
        name="inproj",
    )(x2d, g1, w_bf, cos, sin, qg, kg)


def _attn_kernel(qt_ref, k_ref, vt_ref, km_ref, vmt_ref, o_ref, s_scr, acc_scr):
    tq = qt_ref.shape[3]
    nk = vt_ref.shape[2]
    tk = vt_ref.shape[4]
    unroll = next(u for u in (8, 4, 2) if nk % u == 0)
    qt = jnp.concatenate([qt_ref[0, g] for g in range(Q_PER_KV)], axis=1)

    def scores(j):
        off = pl.multiple_of(j * tk, tk)
        return jnp.dot(k_ref[0, 0, pl.ds(off, tk), :], qt, preferred_element_type=F32)

    s0 = jnp.dot(km_ref[0], qt, preferred_element_type=F32)
    m0 = jnp.max(s0, axis=0, keepdims=True)
    p0 = jnp.exp2(s0 - m0).astype(BF16)
    acc_scr[...] = jnp.dot(vmt_ref[0], p0, preferred_element_type=F32)
    s = scores(0)
    s_scr[0] = s
    mt0 = jnp.max(s, axis=0, keepdims=True)

    def step(j, cur, m, mt, lookahead=True):
        if lookahead:
            s_next = scores(j + 1)
            s_scr[1 - cur] = s_next
            mt_next = jnp.max(s_next, axis=0, keepdims=True)
        else:
            mt_next = mt
        m_new = jnp.maximum(m, mt)
        alpha = jnp.exp2(m - m_new)
        p = jnp.exp2(s_scr[cur] - m_new).astype(BF16)
        pv = jnp.dot(vt_ref[0, 0, j], p, preferred_element_type=F32)
        acc_scr[...] = alpha * acc_scr[...] + pv
        return m_new, mt_next

    def body(jb, carry):
        m, mt = carry
        for u in range(unroll):
            m, mt = step(unroll * jb + u, u % 2, m, mt)
        return m, mt

    m, mt = lax.fori_loop(0, nk // unroll - 1, body, (m0, mt0))
    for u in range(unroll):
        m, mt = step(nk - unroll + u, u % 2, m, mt, lookahead=u < unroll - 1)
    out_t =acc_scr[0:HEAD_DIM, :] / acc_scr[HEAD_DIM:HEAD_DIM + 1, :]
    for g in range(Q_PER_KV):
        o_ref[0, :, g * HEAD_DIM:(g + 1) * HEAD_DIM] = out_t[:, g * tq:(g + 1) * tq].T.astype(BF16)


def _attention(qt, k, vt, km, vmt, tq):
    batch, _, _, seq = qt.shape
    nk, tk = vt.shape[2], vt.shape[4]
    assert nk % 2 == 0
    gw = Q_PER_KV * HEAD_DIM
    return pl.pallas_call(
        _attn_kernel,
        grid=(batch, N_KV_HEADS, seq // tq),
        in_specs=[
            pl.BlockSpec((1, Q_PER_KV, HEAD_DIM, tq), lambda b, h, i: (b, h, 0, i)),
            pl.BlockSpec((1, 1, seq, HEAD_DIM), lambda b, h, i: (b, h, 0, 0)),
            pl.BlockSpec((1, 1, nk, V_ROWS, tk), lambda b, h, i: (b, h, 0, 0, 0)),
            pl.BlockSpec((1, N_META, HEAD_DIM), lambda b, h, i: (h, 0, 0)),
            pl.BlockSpec((1, V_ROWS, N_META), lambda b, h, i: (h, 0, 0)),
        ],
        out_specs=pl.BlockSpec((1, tq, gw), lambda b, h, i: (b, i, h)),
        out_shape=jax.ShapeDtypeStruct((batch, seq, ATTN_WIDTH), BF16),
        scratch_shapes=[pltpu.VMEM((2, tk, Q_PER_KV * tq), F32),
                        pltpu.VMEM((V_ROWS, Q_PER_KV * tq), F32)],
        compiler_params=pltpu.CompilerParams(
            dimension_semantics=("arbitrary", "arbitrary", "arbitrary"),
            vmem_limit_bytes=V7X_VMEM_LIMIT),
        name="attention",
    )(qt, k, vt, km, vmt)


def _post_kernel(attn_ref, pp_ref, prev_ref, next_ref, pm_ref, x_ref, pw_ref, ps_ref, wo_ref,
                 g2_ref, wr_ref, br_ref, h1_ref, b_ref, ri_ref, rg_ref, cnt_ref, ext_ref, mix_ref,
                 *, nj, seq, nsub):
    tm = x_ref.shape[0]
    ts = tm // nsub
    j = pl.program_id(0) % nj
    ext_ref[0:HALO] = jnp.where(j == 0, pm_ref[N_META - HALO:N_META, :], prev_ref[...])
    ext_ref[HALO:HALO + tm] = pp_ref[...]
    ext_ref[HALO + tm:2 * HALO + tm] = jnp.where(j == nj - 1, 0.0, next_ref[...])

    @pl.when(pl.program_id(0) == 0)
    def _():
        cnt_ref[...] = jnp.zeros_like(cnt_ref)

    lane = lax.broadcasted_iota(jnp.int32, (ts, ROUTER_LANES), 1)
    lane_f = lane.astype(F32)
    neg = jnp.float32(-jnp.inf)
    big = jnp.float32(ROUTER_LANES)
    is_group = (lane >= GROUP_LANE0) & (lane < GROUP_LANE0 + N_EXPERT_GROUPS)

    for s in range(nsub):
        r0 = s * ts
        rows = slice(r0, r0 + ts)
        mix_ref[rows, 0:ATTN_WIDTH] = attn_ref[rows, :]
        for g, win in enumerate(POOL_WINDOWS):
            c0, c1 = g * POOL_GROUP, (g + 1) * POOL_GROUP
            half = win // 2
            ext_rows = ts + 2 * HALO
            e = ext_ref[r0:r0 + ext_rows, c0:c1]
            ahead = lambda a, k: pltpu.roll(a, ext_rows - k, 0)
            behind = lambda a, k: pltpu.roll(a, k, 0)
            fwd = e
            k = 1
            while k < half:
                fwd = fwd + ahead(fwd, k)
                k *= 2
            tot = (fwd + behind(fwd, half))[HALO:HALO + ts]
            if s == nsub - 1:
                t8 = j * tm + (tm - HALO) + lax.broadcasted_iota(jnp.int32, (HALO, 1), 0)
                cnt8 = (jnp.minimum(t8 + half, seq) - (t8 - half)).astype(F32)
                mean = jnp.concatenate([tot[0:ts - HALO] * (1.0 / win), tot[ts - HALO:ts] / cnt8], axis=0)
            else:
                mean = tot * (1.0 / win)
            mg = (mean - pp_ref[rows, c0:c1]).astype(BF16)
            yg = jnp.dot(mg, pw_ref[g], preferred_element_type=F32) * ps_ref[:, c0:c1]
            mix_ref[rows, ATTN_WIDTH + c0:ATTN_WIDTH + c1] = yg.astype(BF16)

        h1 = x_ref[rows, :] + jnp.dot(mix_ref[rows, :], wo_ref[...], preferred_element_type=F32)
        h1_ref[rows, :] = h1
        ms = jnp.mean(h1 * h1, axis=-1, keepdims=True)
        bf = h1 * lax.rsqrt(ms + EPS) * g2_ref[...]
        b_ref[rows, :] = bf
        logits = jnp.dot(bf.astype(BF16), wr_ref[...], preferred_element_type=F32) + br_ref[...]

        gl = jnp.where(is_group, logits, neg)
        gmax = jnp.max(gl, axis=1, keepdims=True)
        gidx = jnp.min(jnp.where(gl == gmax, lane_f, big), axis=1, keepdims=True) - GROUP_LANE0
        g_p = 1.0 / jnp.sum(jnp.exp(gl - gmax), axis=1, keepdims=True)
        in_group = (lane // EXPERTS_PER_GROUP).astype(F32) == gidx
        el = jnp.where(in_group & (lane < N_EXPERTS), logits, neg)
        m1 = jnp.max(el, axis=1, keepdims=True)
        i1 = jnp.min(jnp.where(el == m1, lane_f, big), axis=1, keepdims=True)
        el2 = jnp.where(lane_f == i1, neg, el)
        m2 = jnp.max(el2, axis=1, keepdims=True)
        i2 = jnp.min(jnp.where(el2 == m2, lane_f, big), axis=1, keepdims=True)
        p2 = jnp.exp(m2 - m1)
        den = 1.0 + p2
        gate1 = g_p / den
        gate2 = g_p * p2 / den
        ri_ref[rows, :] = jnp.where(lane == 0, i1, jnp.where(lane == 1, i2, 0.0)).astype(jnp.int32)
        rg_ref[rows, :] = jnp.where(lane == 0, gate1, jnp.where(lane == 1, gate2, 0.0))

        chosen = jnp.where(lane_f == i1, 1.0, 0.0) + jnp.where(lane_f == i2, 1.0, 0.0)
        cnt_ref[...] += jnp.broadcast_to(jnp.sum(chosen, axis=0, keepdims=True), cnt_ref.shape)


def _post(attn2d, pp, pm, x2d, pw_bf, ps, wo_bf, g2, wr_bf, br, seq, tm, nsub):
    n, d = x2d.shape
    nj = seq // tm
    hb = tm // HALO
    last_hb = n // HALO - 1
    const2 = lambda i: (0, 0)
    row = lambda i: (i, 0)
    return pl.pallas_call(
        functools.partial(_post_kernel, nj=nj, seq=seq, nsub=nsub),
        grid=(n // tm,),
        in_specs=[
            pl.BlockSpec((tm, ATTN_WIDTH), row),
            pl.BlockSpec((tm, POOL_WIDTH), row),
            pl.BlockSpec((HALO, POOL_WIDTH), lambda i: (jnp.maximum(i * hb - 1, 0), 0)),
            pl.BlockSpec((HALO, POOL_WIDTH), lambda i: (jnp.minimum((i + 1) * hb, last_hb), 0)),
            pl.BlockSpec((N_META, POOL_WIDTH), const2),
            pl.BlockSpec((tm, d), row),
            pl.BlockSpec((N_POOL_GROUPS, POOL_GROUP, POOL_GROUP), lambda i: (0, 0, 0)),
            pl.BlockSpec((1, POOL_WIDTH), const2),
            pl.BlockSpec((ATTN_WIDTH + POOL_WIDTH, d), const2),
            pl.BlockSpec((1, d), const2),
            pl.BlockSpec((d, ROUTER_LANES), const2),
            pl.BlockSpec((1, ROUTER_LANES), const2),
        ],
        out_specs=[
            pl.BlockSpec((tm, d), row),
            pl.BlockSpec((tm, d), row),
            pl.BlockSpec((tm, ROUTER_LANES), row),
            pl.BlockSpec((tm, ROUTER_LANES), row),
            pl.BlockSpec((8, ROUTER_LANES), const2),
        ],
        out_shape=[
            jax.ShapeDtypeStruct((n, d), F32),
            jax.ShapeDtypeStruct((n, d), F32),
            jax.ShapeDtypeStruct((n, ROUTER_LANES), jnp.int32),
            jax.ShapeDtypeStruct((n, ROUTER_LANES), F32),
            jax.ShapeDtypeStruct((8, ROUTER_LANES), F32),
        ],
        scratch_shapes=[pltpu.VMEM((tm + 2 * HALO, POOL_WIDTH), F32),
                        pltpu.VMEM((tm, ATTN_WIDTH + POOL_WIDTH), BF16)],
        compiler_params=pltpu.CompilerParams(
            dimension_semantics=("arbitrary",), vmem_limit_bytes=V7X_VMEM_LIMIT),
        name="post",
    )(attn2d, pp, pp, pp, pm, x2d, pw_bf, ps, wo_bf, g2, wr_bf, br)


def _rank_kernel(ri_ref, cnt_ref, lower_ref, dest_ref, carry, pstart, *, bm):
    t = ri_ref.shape[0]

    @pl.when(pl.program_id(0) == 0)
    def _():
        counts = cnt_ref[...]
        padded = jnp.floor((counts + (bm - 1)) * (1.0 / bm)) * bm
        lane8 = lax.broadcasted_iota(jnp.int32, (8, ROUTER_LANES), 1)
        incl = padded
        shift = 1
        while shift < ROUTER_LANES:
            incl = incl + jnp.where(lane8 >= shift, pltpu.roll(incl, shift, 1), 0.0)
            shift *= 2
        pstart[...] = (incl - padded)[0:1]
        carry[...] = jnp.zeros_like(carry)

    lane = lax.broadcasted_iota(jnp.int32, (t, ROUTER_LANES), 1)
    ri = ri_ref[...]
    oh0 = lane == ri[:, 0:1]
    oh1 = lane == ri[:, 1:2]
    ohf = jnp.where(oh0, 1.0, 0.0) + jnp.where(oh1, 1.0, 0.0)
    prefix = jnp.dot(lower_ref[...], ohf.astype(BF16), preferred_element_type=F32)
    base = prefix + carry[...] + pstart[...]
    d0 = jnp.sum(jnp.where(oh0, base, 0.0), axis=1, keepdims=True)
    d1 = jnp.sum(jnp.where(oh1, base, 0.0), axis=1, keepdims=True)
    dest_ref[...] = jnp.where(lane == 0, d0, jnp.where(lane == 1, d1, 0.0)).astype(jnp.int32)
    carry[...] += jnp.sum(ohf, axis=0, keepdims=True)


def _rank(ri, cnt, bm, t):
    n = ri.shape[0]
    lower = jnp.tril(jnp.ones((t, t), BF16), -1)
    return pl.pallas_call(
        functools.partial(_rank_kernel, bm=bm),
        grid=(n // t,),
        in_specs=[
            pl.BlockSpec((t, ROUTER_LANES), lambda i: (i, 0)),
            pl.BlockSpec((8, ROUTER_LANES), lambda i: (0, 0)),
            pl.BlockSpec((t, t), lambda i: (0, 0)),
        ],
        out_specs=pl.BlockSpec((t, ROUTER_LANES), lambda i: (i, 0)),
        out_shape=jax.ShapeDtypeStruct((n, ROUTER_LANES), jnp.int32),
        scratch_shapes=[pltpu.VMEM((1, ROUTER_LANES), F32), pltpu.VMEM((1, ROUTER_LANES), F32)],
        compiler_params=pltpu.CompilerParams(
            dimension_semantics=("arbitrary",), vmem_limit_bytes=V7X_VMEM_LIMIT),
        name="rank",
    )(ri, cnt, lower)


def _dispatch_kernel(dest_ref, pend_ref, padn_ref, nvalid_ref, b_ref, xs_ref, zeros, sem, zsem, *, bm):
    tm = b_ref.shape[0]
    i = pl.program_id(0)

    @pl.when(i == 0)
    def _():
        zeros[...] = jnp.zeros_like(zeros)
        nblk = xs_ref.shape[0] // bm

        def pad_copy(e, g):
            start = pl.multiple_of(pend_ref[e] - (g + 1) * F32_SUBLANES, F32_SUBLANES)
            return pltpu.make_async_copy(zeros.at[pl.ds(0, F32_SUBLANES)], xs_ref.at[pl.ds(start, F32_SUBLANES)], zsem)

        def tail_copy(blk):
            return pltpu.make_async_copy(zeros, xs_ref.at[pl.ds(pl.multiple_of(blk * bm, bm), bm)], zsem)

        def n_groups(e):
            return (padn_ref[e] + (F32_SUBLANES - 1)) // F32_SUBLANES

        def per_expert(e, c):
            lax.fori_loop(0, n_groups(e), lambda g, c2: (pad_copy(e, g).start(), c2)[1], 0)
            return c
        lax.fori_loop(0, N_EXPERTS, per_expert, 0)
        lax.fori_loop(nvalid_ref[0], nblk, lambda blk, c: (tail_copy(blk).start(), c)[1], 0)

        def per_expert_wait(e, c):
            lax.fori_loop(0, n_groups(e), lambda g, c2: (pad_copy(e, g).wait(), c2)[1], 0)
            return c
        lax.fori_loop(0, N_EXPERTS, per_expert_wait, 0)
        lax.fori_loop(nvalid_ref[0], nblk, lambda blk, c: (tail_copy(blk).wait(), c)[1], 0)

    def row_copy(r, k):
        p = dest_ref[(i * tm + r) * TOP_K + k]
        return pltpu.make_async_copy(b_ref.at[pl.ds(r, 1)], xs_ref.at[pl.ds(p, 1)], sem)

    def issue(r8, c):
        for u in range(DMA_UNROLL):
            for k in range(TOP_K):
                row_copy(r8 * DMA_UNROLL + u, k).start()
        return c
    lax.fori_loop(0, tm // DMA_UNROLL, issue, 0)

    def drain(r8, c):
        for u in range(DMA_UNROLL):
            for k in range(TOP_K):
                row_copy(r8 * DMA_UNROLL + u, k).wait()
        return c
    lax.fori_loop(0, tm // DMA_UNROLL, drain, 0)


def _dispatch(dest, pend, padn, nvalid, b_pk, n_slots, tm, bm):
    n, hw = b_pk.shape
    grid_spec = pltpu.PrefetchScalarGridSpec(
        num_scalar_prefetch=4,
        grid=(n // tm,),
        in_specs=[pl.BlockSpec((tm, hw), lambda i, *_: (i, 0))],
        out_specs=pl.BlockSpec(memory_space=pl.ANY),
        scratch_shapes=[pltpu.VMEM((bm, hw), F32), pltpu.SemaphoreType.DMA(()),
                        pltpu.SemaphoreType.DMA(())],
    )
    return pl.pallas_call(
        functools.partial(_dispatch_kernel, bm=bm),
        grid_spec=grid_spec,
        out_shape=jax.ShapeDtypeStruct((n_slots, hw), F32),
        compiler_params=pltpu.CompilerParams(
            dimension_semantics=("arbitrary",), vmem_limit_bytes=V7X_VMEM_LIMIT,
            has_side_effects=True),
        name="dispatch",
    )(dest, pend, padn, nvalid, b_pk)


def _expert_kernel(blk_e_ref, nvalid_ref, xs_ref, wg_ref, wu_ref, wd_ref, y_ref,
                   wg_bf, wu_bf, wd_bf):
    i = pl.program_id(0)
    nvalid = nvalid_ref[0]
    new_expert = jnp.logical_or(i == 0, blk_e_ref[i] != blk_e_ref[jnp.maximum(i - 1, 0)])

    @pl.when(jnp.logical_and(new_expert, i < nvalid))
    def _():
        wg_bf[...] = wg_ref[0].astype(BF16)
        wu_bf[...] = wu_ref[0].astype(BF16)
        wd_bf[...] = wd_ref[0].astype(BF16)

    @pl.when(i < nvalid)
    def _():
        xb = xs_ref[...].astype(BF16)
        hg = jnp.dot(xb, wg_bf[...], preferred_element_type=F32)
        hu = jnp.dot(xb, wu_bf[...], preferred_element_type=F32)
        h = (hg * jax.nn.sigmoid(hg) * hu).astype(BF16)
        y_ref[...] = jnp.dot(h, wd_bf[...], preferred_element_type=F32)

    @pl.when(i >= nvalid)
    def _():
        y_ref[...] = jnp.zeros_like(y_ref)


def _experts(blk_e, nvalid, xs, w_gate, w_up, w_down, bm):
    n_slots, hw = xs.shape
    nblk = n_slots // bm
    _, d, de = w_gate.shape
    last = lambda i, nv: jnp.maximum(jnp.minimum(i, nv[0] - 1), 0)
    grid_spec = pltpu.PrefetchScalarGridSpec(
        num_scalar_prefetch=2,
        grid=(nblk,),
        in_specs=[
            pl.BlockSpec((bm, hw), lambda i, be, nv: (last(i, nv), 0)),
            pl.BlockSpec((1, d, de), lambda i, be, nv: (be[last(i, nv)], 0, 0)),
            pl.BlockSpec((1, d, de), lambda i, be, nv: (be[last(i, nv)], 0, 0)),
            pl.BlockSpec((1, de, d), lambda i, be, nv: (be[last(i, nv)], 0, 0)),
        ],
        out_specs=pl.BlockSpec((bm, hw), lambda i, be, nv: (i, 0)),
        scratch_shapes=[pltpu.VMEM((d, de), BF16), pltpu.VMEM((d, de), BF16),
                        pltpu.VMEM((de, d), BF16)],
    )
    return pl.pallas_call(
        _expert_kernel,
        grid_spec=grid_spec,
        out_shape=jax.ShapeDtypeStruct((n_slots, hw), F32),
        compiler_params=pltpu.CompilerParams(
            dimension_semantics=("arbitrary",), vmem_limit_bytes=V7X_VMEM_LIMIT),
        name="experts",
    )(blk_e, nvalid, xs, w_gate, w_up, w_down)


def _combine_kernel(dest_ref, h1_ref, rg_ref, y_hbm, out_ref, ybuf, sem):
    tm = h1_ref.shape[0]
    i = pl.program_id(0)
    n_steps = pl.num_programs(0)
    slot = i % 2

    def row_copy(tile, r, k, s):
        p = dest_ref[(tile * tm + r) * TOP_K + k]
        return pltpu.make_async_copy(y_hbm.at[pl.ds(p, 1)], ybuf.at[s, k, pl.ds(r, 1)], sem.at[s])

    def issue(tile, s):
        def body(r8, c):
            for u in range(DMA_UNROLL):
                for k in range(TOP_K):
                    row_copy(tile, r8 * DMA_UNROLL + u, k, s).start()
            return c
        lax.fori_loop(0, tm // DMA_UNROLL, body, 0)

    @pl.when(i == 0)
    def _():
        issue(0, 0)

    @pl.when(i + 1 < n_steps)
    def _():
        issue(i + 1, 1 - slot)

    def drain(r8, c):
        for u in range(DMA_UNROLL):
            for k in range(TOP_K):
                row_copy(i, r8 * DMA_UNROLL + u, k, slot).wait()
        return c
    lax.fori_loop(0, tm // DMA_UNROLL, drain, 0)

    rg = rg_ref[...]
    out = h1_ref[...]
    for k in range(TOP_K):
        out = out + rg[:, k:k + 1] * ybuf[slot, k]
    out_ref[...] = out


def _combine(dest, h1, rg, y, tm):
    n, d = h1.shape
    hw = y.shape[1]
    row = lambda i, de: (i, 0)
    grid_spec = pltpu.PrefetchScalarGridSpec(
        num_scalar_prefetch=1,
        grid=(n // tm,),
        in_specs=[
            pl.BlockSpec((tm, d), row),
            pl.BlockSpec((tm, ROUTER_LANES), row),
            pl.BlockSpec(memory_space=pl.ANY),
        ],
        out_specs=pl.BlockSpec((tm, d), row),
        scratch_shapes=[pltpu.VMEM((2, TOP_K, tm, hw), F32), pltpu.SemaphoreType.DMA((2,))],
    )
    return pl.pallas_call(
        _combine_kernel,
        grid_spec=grid_spec,
        out_shape=jax.ShapeDtypeStruct((n, d), F32),
        compiler_params=pltpu.CompilerParams(
            dimension_semantics=("arbitrary",), vmem_limit_bytes=V7X_VMEM_LIMIT),
        name="combine",
    )(dest, h1, rg, y)


def _rope_tables(seq):
    rows = seq // GRID_W
    inv_freq = jnp.power(ROPE_THETA, -jnp.arange(0, ROPE_AXIS_DIM, 2, dtype=F32) / ROPE_AXIS_DIM)
    ang_r = jnp.arange(rows).astype(F32)[:, None] * inv_freq[None, :]
    ang_c = jnp.arange(GRID_W).astype(F32)[:, None] * inv_freq[None, :]
    shape = (rows, GRID_W, ROPE_AXIS_DIM // 2)
    by_row = lambda a: jnp.broadcast_to(a[:, None, :], shape)
    by_col = lambda a: jnp.broadcast_to(a[None, :, :], shape)
    cr, sr = by_row(jnp.cos(ang_r)), by_row(jnp.sin(ang_r))
    cc, sc = by_col(jnp.cos(ang_c)), by_col(jnp.sin(ang_c))
    cos = jnp.concatenate([cr, cr, cc, cc], axis=-1).reshape(seq, HEAD_DIM)
    sin = jnp.concatenate([-sr, sr, -sc, sc], axis=-1).reshape(seq, HEAD_DIM)
    return cos, sin


def _block_plan(counts, nblk, bm):
    padded = (counts + bm - 1) // bm * bm
    pends = jnp.cumsum(padded)
    blk_start = jnp.arange(nblk, dtype=jnp.int32) * bm
    blk_e = jnp.minimum(jnp.sum(pends[None, :] <= blk_start[:, None], axis=1), N_EXPERTS - 1)
    nvalid = (pends[-1] // bm).reshape(1)
    padn = padded - counts
    return (blk_e.astype(jnp.int32), nvalid.astype(jnp.int32), pends.astype(jnp.int32),
            padn.astype(jnp.int32))


def kernel(x, meta_tokens, norm1_g, w_in, q_norm_g, k_norm_g, pool_w, pool_scale, w_out, norm2_g,
           w_router_group, b_router_group, w_router_expert, b_router_expert, w_gate, w_up, w_down):
    batch, seq, d = x.shape
    n = batch * seq
    assert norm1_g.shape[0] == 1, "single-layer block"
    assert seq % GRID_W == 0 and seq % 256 == 0

    tm_in = _tile(seq // 2, 512)
    tm_post = _tile(seq, 512)
    tq = _tile(seq, 256)
    bm = 256
    tm_moe = _tile(seq, 512)

    x2d = x.reshape(n, d)
    w_in_bf = w_in[0].astype(BF16)
    cos, sin = _rope_tables(seq)
    qg = q_norm_g[0].reshape(1, HEAD_DIM)
    kg = k_norm_g[0].reshape(1, HEAD_DIM)
    g1 = norm1_g[0].reshape(1, d)

    qt, k, vt, pp = _inproj(x2d, g1, w_in_bf, cos, sin, qg, kg, batch, seq, tm_in)
    ones = jnp.ones((N_META, HEAD_DIM), F32)
    _, km, vmt, pm = _inproj(meta_tokens.astype(F32), g1, w_in_bf, ones, jnp.zeros_like(ones),
                             qg, kg, 1, N_META, N_META)

    attn = _attention(qt, k, vt, km[0], vmt[0, :, 0], tq)

    wr = jnp.concatenate([w_router_expert[0], w_router_group[0],
                          jnp.zeros((d, ROUTER_LANES - N_EXPERTS - N_EXPERT_GROUPS), F32)], axis=1)
    br = jnp.concatenate([b_router_expert[0], b_router_group[0],
                          jnp.zeros((ROUTER_LANES - N_EXPERTS - N_EXPERT_GROUPS,), F32)])
    h1, b_pk, ri, rg, cnt = _post(attn.reshape(n, ATTN_WIDTH), pp, pm, x2d, pool_w[0].astype(BF16),
                             pool_scale[0].reshape(1, POOL_WIDTH), w_out[0].astype(BF16),
                             norm2_g[0].reshape(1, d), wr.astype(BF16),
                             br.reshape(1, ROUTER_LANES), seq, tm_post, 2)

    nblk = -(-(n * TOP_K) // bm) + N_EXPERTS
    dest = _rank(ri, cnt, bm, _tile(seq, 1024))[:, :TOP_K].reshape(-1)
    blk_e, nvalid, pends, padn = _block_plan(cnt[0, :N_EXPERTS].astype(jnp.int32), nblk, bm)
    xs = _dispatch(dest, pends, padn, nvalid, b_pk, nblk * bm, tm_moe, bm)
    y = _experts(blk_e, nvalid, xs, w_gate[0], w_up[0], w_down[0], bm)
    out = _combine(dest, h1, rg, y, tm_moe)
    return out.reshape(batch, seq, d)
```

```python
import functools
import math

import jax
import jax.numpy as jnp
from jax import lax
from jax.experimental import pallas as pl
from jax.experimental.pallas import tpu as pltpu

N_META = 16
GRID_W = 64
HEAD_DIM = 128
N_Q_HEADS = 8
N_KV_HEADS = 2
Q_PER_KV = N_Q_HEADS // N_KV_HEADS
ATTN_WIDTH = N_Q_HEADS * HEAD_DIM
KV_WIDTH = N_KV_HEADS * HEAD_DIM
POOL_WINDOWS = (2, 4, 8, 16)
N_POOL_GROUPS = len(POOL_WINDOWS)
POOL_GROUP = 256
POOL_WIDTH = N_POOL_GROUPS * POOL_GROUP
ROPE_THETA = 10000.0
ROPE_AXIS_DIM = HEAD_DIM // 2
N_EXPERT_GROUPS = 4
EXPERTS_PER_GROUP = 8
N_EXPERTS = N_EXPERT_GROUPS * EXPERTS_PER_GROUP
TOP_K = 2
EPS = 1e-6

V_ONES_ROWS = 16
V_ROWS = HEAD_DIM + V_ONES_ROWS
F32_SUBLANES = 8
HALO = F32_SUBLANES
ROUTER_LANES = 128
GROUP_LANE0 = N_EXPERTS
DMA_UNROLL = 16
LOG2E = math.log2(math.e)
Q_SCALE = HEAD_DIM ** -0.5 * LOG2E

V7X_VMEM_LIMIT = 56 * 1024 * 1024

F32 = jnp.float32
BF16 = jnp.bfloat16


def _tile(n, pref):
    t = min(n, pref)
    while n % t:
        t //= 2
    return t


def _inproj_kernel(x_ref, g1_ref, w_ref, crow_ref, srow_ref, ccol_ref, scol_ref, qg_ref, kg_ref,
                   qt_ref, k_ref, vt_ref, p_ref, *, nsub, rope):
    tm = x_ref.shape[0]
    ts = tm // nsub
    lane = lax.broadcasted_iota(jnp.int32, (ts, HEAD_DIM), 1)
    first_half = (lane % ROPE_AXIS_DIM) < (ROPE_AXIS_DIM // 2)
    qg = qg_ref[...]
    kg = kg_ref[...]

    for s in range(nsub):
        rows = slice(s * ts, (s + 1) * ts)
        xf = x_ref[rows, :]
        ms = jnp.mean(xf * xf, axis=-1, keepdims=True)
        a = (xf * lax.rsqrt(ms + EPS) * g1_ref[...]).astype(BF16)
        if rope:
            g0 = s * ts // GRID_W
            cos = jnp.concatenate([crow_ref[0, g:g + 1, :] + ccol_ref[...]
                                   for g in range(g0, g0 + ts // GRID_W)], axis=0)
            sin = jnp.concatenate([srow_ref[0, g:g + 1, :] + scol_ref[...]
                                   for g in range(g0, g0 + ts // GRID_W)], axis=0)

        def norm_rope(hd, g, scale):
            n = hd * lax.rsqrt(jnp.mean(hd * hd, axis=-1, keepdims=True) + EPS) * g
            if not rope:
                return n * scale
            partner = jnp.where(first_half,
                                pltpu.roll(n, HEAD_DIM - ROPE_AXIS_DIM // 2, 1),
                                pltpu.roll(n, ROPE_AXIS_DIM // 2, 1))
            return (n * cos + partner * sin) * scale

        for c in range(N_Q_HEADS // 2):
            pr = jnp.dot(a, w_ref[:, c * 256:(c + 1) * 256], preferred_element_type=F32)
            for j in range(2):
                qh = norm_rope(pr[:, j * HEAD_DIM:(j + 1) * HEAD_DIM], qg, Q_SCALE)
                qt_ref[0, 2 * c + j, :, rows] = qh.T.astype(BF16)
        pr = jnp.dot(a, w_ref[:, ATTN_WIDTH:ATTN_WIDTH + KV_WIDTH], preferred_element_type=F32)
        for j in range(N_KV_HEADS):
            k_ref[0, j, rows, :] = norm_rope(pr[:, j * HEAD_DIM:(j + 1) * HEAD_DIM], kg, 1.0).astype(BF16)
        pr = jnp.dot(a, w_ref[:, ATTN_WIDTH + KV_WIDTH:ATTN_WIDTH + 2 * KV_WIDTH],
                     preferred_element_type=F32)
        for j in range(N_KV_HEADS):
            vt_ref[0, j, 0, 0:HEAD_DIM, rows] = pr[:, j * HEAD_DIM:(j + 1) * HEAD_DIM].T.astype(BF16)
            vt_ref[0, j, 0, HEAD_DIM:V_ROWS, rows] = jnp.ones((V_ONES_ROWS, ts), BF16)
        p_ref[rows, :] = jnp.dot(a, w_ref[:, ATTN_WIDTH + 2 * KV_WIDTH:], preferred_element_type=F32)


def _inproj(x2d, g1, w_bf, tables, qg, kg, batch, seq, tm, nsub, rope):
    n, d = x2d.shape
    nj = seq // tm
    in_w = w_bf.shape[1]
    crow, srow, ccol, scol = tables
    gr = crow.shape[1]
    const = lambda i: (0, 0)
    return pl.pallas_call(
        functools.partial(_inproj_kernel, nsub=nsub, rope=rope),
        grid=(n // tm,),
        in_specs=[
            pl.BlockSpec((tm, d), lambda i: (i, 0)),
            pl.BlockSpec((1, d), const),
            pl.BlockSpec((d, in_w), const),
            pl.BlockSpec((1, gr, HEAD_DIM), lambda i: (i % nj, 0, 0)),
            pl.BlockSpec((1, gr, HEAD_DIM), lambda i: (i % nj, 0, 0)),
            pl.BlockSpec((GRID_W, HEAD_DIM), const),
            pl.BlockSpec((GRID_W, HEAD_DIM), const),
            pl.BlockSpec((1, HEAD_DIM), const),
            pl.BlockSpec((1, HEAD_DIM), const),
        ],
        out_specs=[
            pl.BlockSpec((1, N_Q_HEADS, HEAD_DIM, tm), lambda i: (i // nj, 0, 0, i % nj)),
            pl.BlockSpec((1, N_KV_HEADS, tm, HEAD_DIM), lambda i: (i // nj, 0, i % nj, 0)),
            pl.BlockSpec((1, N_KV_HEADS, 1, V_ROWS, tm), lambda i: (i // nj, 0, i % nj, 0, 0)),
            pl.BlockSpec((tm, POOL_WIDTH), lambda i: (i, 0)),
        ],
        out_shape=[
            jax.ShapeDtypeStruct((batch, N_Q_HEADS, HEAD_DIM, seq), BF16),
            jax.ShapeDtypeStruct((batch, N_KV_HEADS, seq, HEAD_DIM), BF16),
            jax.ShapeDtypeStruct((batch, N_KV_HEADS, nj, V_ROWS, tm), BF16),
            jax.ShapeDtypeStruct((n, POOL_WIDTH), F32),
        ],
        compiler_params=pltpu.CompilerParams(
            dimension_semantics=("arbitrary",), vmem_limit_bytes=V7X_VMEM_LIMIT),
        name="inproj",
    )(x2d, g1, w_bf, crow, srow, ccol, scol, qg, kg)


def _attn_kernel(qt_ref, k_ref, vt_ref, km_ref, vmt_ref, o_ref, s_scr, acc_scr):
    tq = qt_ref.shape[3]
    nk = vt_ref.shape[2]
    tk = vt_ref.shape[4]
    unroll = next(u for u in (8, 4, 2) if nk % u == 0)
    qt = jnp.concatenate([qt_ref[0, g] for g in range(Q_PER_KV)], axis=1)

    def scores(j):
        off = pl.multiple_of(j * tk, tk)
        return jnp.dot(k_ref[0, 0, pl.ds(off, tk), :], qt, preferred_element_type=F32)

    s0 = jnp.dot(km_ref[0], qt, preferred_element_type=F32)
    m0 = jnp.max(s0, axis=0, keepdims=True)
    p0 = jnp.exp2(s0 - m0).astype(BF16)
    acc_scr[...] = jnp.dot(vmt_ref[0], p0, preferred_element_type=F32)
    s = scores(0)
    s_scr[0] = s
    mt0 = jnp.max(s, axis=0, keepdims=True)

    def step(j, cur, m, mt, lookahead=True):
        if lookahead:
            s_next = scores(j + 1)
            s_scr[1 - cur] = s_next
            mt_next = jnp.max(s_next, axis=0, keepdims=True)
        else:
            mt_next = mt
        m_new = jnp.maximum(m, mt)
        alpha = jnp.exp2(m - m_new)
        p = jnp.exp2(s_scr[cur] - m_new).astype(BF16)
        pv = jnp.dot(vt_ref[0, 0, j], p, preferred_element_type=F32)
        acc_scr[...] = alpha * acc_scr[...] + pv
        return m_new, mt_next

    def body(jb, carry):
        m, mt = carry
        for u in range(unroll):
            m, mt = step(unroll * jb + u, u % 2, m, mt)
        return m, mt

    m, mt = lax.fori_loop(0, nk // unroll - 1, body, (m0, mt0))
    for u in range(unroll):
        m, mt = step(nk - unroll + u, u % 2, m, mt, lookahead=u < unroll - 1)
    out_t = acc_scr[0:HEAD_DIM, :] / acc_scr[HEAD_DIM:HEAD_DIM + 1, :]
    for g in range(Q_PER_KV):
        o_ref[0, :, g * HEAD_DIM:(g + 1) * HEAD_DIM] = out_t[:, g * tq:(g + 1) * tq].T.astype(BF16)


def _attention(qt, k, vt, km, vmt, tq):
    batch, _, _, seq = qt.shape
    nk, tk = vt.shape[2], vt.shape[4]
    assert nk % 2 == 0
    gw = Q_PER_KV * HEAD_DIM
    return pl.pallas_call(
        _attn_kernel,
        grid=(batch, N_KV_HEADS, seq // tq),
        in_specs=[
            pl.BlockSpec((1, Q_PER_KV, HEAD_DIM, tq), lambda b, h, i: (b, h, 0, i)),
            pl.BlockSpec((1, 1, seq, HEAD_DIM), lambda b, h, i: (b, h, 0, 0)),
            pl.BlockSpec((1, 1, nk, V_ROWS, tk), lambda b, h, i: (b, h, 0, 0, 0)),
            pl.BlockSpec((1, N_META, HEAD_DIM), lambda b, h, i: (h, 0, 0)),
            pl.BlockSpec((1, V_ROWS, N_META), lambda b, h, i: (h, 0, 0)),
        ],
        out_specs=pl.BlockSpec((1, tq, gw), lambda b, h, i: (b, i, h)),
        out_shape=jax.ShapeDtypeStruct((batch, seq, ATTN_WIDTH), BF16),
        scratch_shapes=[pltpu.VMEM((2, tk, Q_PER_KV * tq), F32),
                        pltpu.VMEM((V_ROWS, Q_PER_KV * tq), F32)],
        compiler_params=pltpu.CompilerParams(
            dimension_semantics=("arbitrary", "arbitrary", "arbitrary"),
            vmem_limit_bytes=V7X_VMEM_LIMIT),
        name="attention",
    )(qt, k, vt, km, vmt)


def _post_kernel(attn_ref, pp_ref, prev_ref, next_ref, pm_ref, x_ref, pw_ref, ps_ref, wo_ref,
                 g2_ref, wr_ref, br_ref, h1_ref, b_ref, ri_ref, rg_ref, cnt_ref, ext_ref, mix_ref,
                 *, nj, seq, nsub):
    tm = x_ref.shape[0]
    ts = tm // nsub
    j = pl.program_id(0) % nj
    ext_ref[0:HALO] = jnp.where(j == 0, pm_ref[N_META - HALO:N_META, :], prev_ref[...])
    ext_ref[HALO:HALO + tm] = pp_ref[...]
    ext_ref[HALO + tm:2 * HALO + tm] = jnp.where(j == nj - 1, 0.0, next_ref[...])

    @pl.when(pl.program_id(0) == 0)
    def _():
        cnt_ref[...] = jnp.zeros_like(cnt_ref)

    lane = lax.broadcasted_iota(jnp.int32, (ts, ROUTER_LANES), 1)
    lane_f = lane.astype(F32)
    neg = jnp.float32(-jnp.inf)
    big = jnp.float32(ROUTER_LANES)
    is_group = (lane >= GROUP_LANE0) & (lane < GROUP_LANE0 + N_EXPERT_GROUPS)

    for s in range(nsub):
        r0 = s * ts
        rows = slice(r0, r0 + ts)
        mix_ref[rows, 0:ATTN_WIDTH] = attn_ref[rows, :]
        for g, win in enumerate(POOL_WINDOWS):
            c0, c1 = g * POOL_GROUP, (g + 1) * POOL_GROUP
            half = win // 2
            ext_rows = ts + 2 * HALO
            e = ext_ref[r0:r0 + ext_rows, c0:c1]
            ahead = lambda a, k: pltpu.roll(a, ext_rows - k, 0)
            behind = lambda a, k: pltpu.roll(a, k, 0)
            fwd = e
            k = 1
            while k < half:
                fwd = fwd + ahead(fwd, k)
                k *= 2
            tot = (fwd + behind(fwd, half))[HALO:HALO + ts]
            if s == nsub - 1:
                t8 = j * tm + (tm - HALO) + lax.broadcasted_iota(jnp.int32, (HALO, 1), 0)
                cnt8 = (jnp.minimum(t8 + half, seq) - (t8 - half)).astype(F32)
                mean = jnp.concatenate([tot[0:ts - HALO] * (1.0 / win), tot[ts - HALO:ts] / cnt8], axis=0)
            else:
                mean = tot * (1.0 / win)
            mg = (mean - pp_ref[rows, c0:c1]).astype(BF16)
            yg = jnp.dot(mg, pw_ref[g], preferred_element_type=F32) * ps_ref[:, c0:c1]
            mix_ref[rows, ATTN_WIDTH + c0:ATTN_WIDTH + c1] = yg.astype(BF16)

        h1 = x_ref[rows, :] + jnp.dot(mix_ref[rows, :], wo_ref[...], preferred_element_type=F32)
        h1_ref[rows, :] = h1
        ms = jnp.mean(h1 * h1, axis=-1, keepdims=True)
        bf = h1 * lax.rsqrt(ms + EPS) * g2_ref[...]
        b_ref[rows, :] = bf
        logits = jnp.dot(bf.astype(BF16), wr_ref[...], preferred_element_type=F32) + br_ref[...]

        gl = jnp.where(is_group, logits, neg)
        gmax = jnp.max(gl, axis=1, keepdims=True)
        gidx = jnp.min(jnp.where(gl == gmax, lane_f, big), axis=1, keepdims=True) - GROUP_LANE0
        g_p = 1.0 / jnp.sum(jnp.exp(gl - gmax), axis=1, keepdims=True)
        in_group = (lane // EXPERTS_PER_GROUP).astype(F32) == gidx
        el = jnp.where(in_group & (lane < N_EXPERTS), logits, neg)
        m1 = jnp.max(el, axis=1, keepdims=True)
        i1 = jnp.min(jnp.where(el == m1, lane_f, big), axis=1, keepdims=True)
        el2 = jnp.where(lane_f == i1, neg, el)
        m2 = jnp.max(el2, axis=1, keepdims=True)
        i2 = jnp.min(jnp.where(el2 == m2, lane_f, big), axis=1, keepdims=True)
        p2 = jnp.exp(m2 - m1)
        den = 1.0 + p2
        gate1 = g_p / den
        gate2 = g_p * p2 / den
        ri_ref[rows, :] = jnp.where(lane == 0, i1, jnp.where(lane == 1, i2, 0.0)).astype(jnp.int32)
        rg_ref[rows, :] = jnp.where(lane == 0, gate1, jnp.where(lane == 1, gate2, 0.0))

        chosen = jnp.where(lane_f == i1, 1.0, 0.0) + jnp.where(lane_f == i2, 1.0, 0.0)
        cnt_ref[...] += jnp.broadcast_to(jnp.sum(chosen, axis=0, keepdims=True), cnt_ref.shape)


def _post(attn2d, pp, pm, x2d, pw_bf, ps, wo_bf, g2, wr_bf, br, seq, tm, nsub):
    n, d = x2d.shape
    nj = seq // tm
    hb = tm // HALO
    last_hb = n // HALO - 1
    const2 = lambda i: (0, 0)
    row = lambda i: (i, 0)
    return pl.pallas_call(
        functools.partial(_post_kernel, nj=nj, seq=seq, nsub=nsub),
        grid=(n // tm,),
        in_specs=[
            pl.BlockSpec((tm, ATTN_WIDTH), row),
            pl.BlockSpec((tm, POOL_WIDTH), row),
            pl.BlockSpec((HALO, POOL_WIDTH), lambda i: (jnp.maximum(i * hb - 1, 0), 0)),
            pl.BlockSpec((HALO, POOL_WIDTH), lambda i: (jnp.minimum((i + 1) * hb, last_hb), 0)),
            pl.BlockSpec((N_META, POOL_WIDTH), const2),
            pl.BlockSpec((tm, d), row),
            pl.BlockSpec((N_POOL_GROUPS, POOL_GROUP, POOL_GROUP), lambda i: (0, 0, 0)),
            pl.BlockSpec((1, POOL_WIDTH), const2),
            pl.BlockSpec((ATTN_WIDTH + POOL_WIDTH, d), const2),
            pl.BlockSpec((1, d), const2),
            pl.BlockSpec((d, ROUTER_LANES), const2),
            pl.BlockSpec((1, ROUTER_LANES), const2),
        ],
        out_specs=[
            pl.BlockSpec((tm, d), row),
            pl.BlockSpec((tm, d), row),
            pl.BlockSpec((tm, ROUTER_LANES), row),
            pl.BlockSpec((tm, ROUTER_LANES), row),
            pl.BlockSpec((8, ROUTER_LANES), const2),
        ],
        out_shape=[
            jax.ShapeDtypeStruct((n, d), F32),
            jax.ShapeDtypeStruct((n, d), F32),
            jax.ShapeDtypeStruct((n, ROUTER_LANES), jnp.int32),
            jax.ShapeDtypeStruct((n, ROUTER_LANES), F32),
            jax.ShapeDtypeStruct((8, ROUTER_LANES), F32),
        ],
        scratch_shapes=[pltpu.VMEM((tm + 2 * HALO, POOL_WIDTH), F32),
                        pltpu.VMEM((tm, ATTN_WIDTH + POOL_WIDTH), BF16)],
        compiler_params=pltpu.CompilerParams(
            dimension_semantics=("arbitrary",), vmem_limit_bytes=V7X_VMEM_LIMIT),
        name="post",
    )(attn2d, pp, pp, pp, pm, x2d, pw_bf, ps, wo_bf, g2, wr_bf, br)


def _rank_kernel(ri_ref, cnt_ref, lower_ref, dest_ref, carry, pstart, *, bm):
    t = ri_ref.shape[0]

    @pl.when(pl.program_id(0) == 0)
    def _():
        counts = cnt_ref[...]
        padded = jnp.floor((counts + (bm - 1)) * (1.0 / bm)) * bm
        lane8 = lax.broadcasted_iota(jnp.int32, (8, ROUTER_LANES), 1)
        incl = padded
        shift = 1
        while shift < ROUTER_LANES:
            incl = incl + jnp.where(lane8 >= shift, pltpu.roll(incl, shift, 1), 0.0)
            shift *= 2
        pstart[...] = (incl - padded)[0:1]
        carry[...] = jnp.zeros_like(carry)

    lane = lax.broadcasted_iota(jnp.int32, (t, ROUTER_LANES), 1)
    ri = ri_ref[...]
    oh0 = lane == ri[:, 0:1]
    oh1 = lane == ri[:, 1:2]
    ohf = jnp.where(oh0, 1.0, 0.0) + jnp.where(oh1, 1.0, 0.0)
    prefix = jnp.dot(lower_ref[...], ohf.astype(BF16), preferred_element_type=F32)
    base = prefix + carry[...] + pstart[...]
    d0 = jnp.sum(jnp.where(oh0, base, 0.0), axis=1, keepdims=True)
    d1 = jnp.sum(jnp.where(oh1, base, 0.0), axis=1, keepdims=True)
    dest_ref[...] = jnp.where(lane == 0, d0, jnp.where(lane == 1, d1, 0.0)).astype(jnp.int32)
    carry[...] += jnp.sum(ohf, axis=0, keepdims=True)


def _rank(ri, cnt, bm, t):
    n = ri.shape[0]
    lower = jnp.tril(jnp.ones((t, t), BF16), -1)
    return pl.pallas_call(
        functools.partial(_rank_kernel, bm=bm),
        grid=(n // t,),
        in_specs=[
            pl.BlockSpec((t, ROUTER_LANES), lambda i: (i, 0)),
            pl.BlockSpec((8, ROUTER_LANES), lambda i: (0, 0)),
            pl.BlockSpec((t, t), lambda i: (0, 0)),
        ],
        out_specs=pl.BlockSpec((t, ROUTER_LANES), lambda i: (i, 0)),
        out_shape=jax.ShapeDtypeStruct((n, ROUTER_LANES), jnp.int32),
        scratch_shapes=[pltpu.VMEM((1, ROUTER_LANES), F32), pltpu.VMEM((1, ROUTER_LANES), F32)],
        compiler_params=pltpu.CompilerParams(
            dimension_semantics=("arbitrary",), vmem_limit_bytes=V7X_VMEM_LIMIT),
        name="rank",
    )(ri, cnt, lower)


def _dispatch_kernel(dest_ref, pend_ref, padn_ref, nvalid_ref, b_ref, xs_ref, zeros, sem, zsem, *, bm):
    tm = b_ref.shape[0]
    i = pl.program_id(0)

    @pl.when(i == 0)
    def _():
        zeros[...] = jnp.zeros_like(zeros)
        nblk = xs_ref.shape[0] // bm

        def pad_copy(e, g):
            start = pl.multiple_of(pend_ref[e] - (g + 1) * F32_SUBLANES, F32_SUBLANES)
            return pltpu.make_async_copy(zeros.at[pl.ds(0, F32_SUBLANES)], xs_ref.at[pl.ds(start, F32_SUBLANES)], zsem)

        def tail_copy(blk):
            return pltpu.make_async_copy(zeros, xs_ref.at[pl.ds(pl.multiple_of(blk * bm, bm), bm)], zsem)

        def n_groups(e):
            return (padn_ref[e] + (F32_SUBLANES - 1)) // F32_SUBLANES

        def per_expert(e, c):
            lax.fori_loop(0, n_groups(e), lambda g, c2: (pad_copy(e, g).start(), c2)[1], 0)
            return c
        lax.fori_loop(0, N_EXPERTS, per_expert, 0)
        lax.fori_loop(nvalid_ref[0], nblk, lambda blk, c: (tail_copy(blk).start(), c)[1], 0)

        def per_expert_wait(e, c):
            lax.fori_loop(0, n_groups(e), lambda g, c2: (pad_copy(e, g).wait(), c2)[1], 0)
            return c
        lax.fori_loop(0, N_EXPERTS, per_expert_wait, 0)
        lax.fori_loop(nvalid_ref[0], nblk, lambda blk, c: (tail_copy(blk).wait(), c)[1], 0)

    def row_copy(r, k):
        p = dest_ref[(i * tm + r) * TOP_K + k]
        return pltpu.make_async_copy(b_ref.at[pl.ds(r, 1)], xs_ref.at[pl.ds(p, 1)], sem)

    def issue(r8, c):
        for u in range(DMA_UNROLL):
            for k in range(TOP_K):
                row_copy(r8 * DMA_UNROLL + u, k).start()
        return c
    lax.fori_loop(0, tm // DMA_UNROLL, issue, 0)

    def drain(r8, c):
        for u in range(DMA_UNROLL):
            for k in range(TOP_K):
                row_copy(r8 * DMA_UNROLL + u, k).wait()
        return c
    lax.fori_loop(0, tm // DMA_UNROLL, drain, 0)


def _dispatch(dest, pend, padn, nvalid, b_pk, n_slots, tm, bm):
    n, hw = b_pk.shape
    grid_spec = pltpu.PrefetchScalarGridSpec(
        num_scalar_prefetch=4,
        grid=(n // tm,),
        in_specs=[pl.BlockSpec((tm, hw), lambda i, *_: (i, 0))],
        out_specs=pl.BlockSpec(memory_space=pl.ANY),
        scratch_shapes=[pltpu.VMEM((bm, hw), F32), pltpu.SemaphoreType.DMA(()),
                        pltpu.SemaphoreType.DMA(())],
    )
    return pl.pallas_call(
        functools.partial(_dispatch_kernel, bm=bm),
        grid_spec=grid_spec,
        out_shape=jax.ShapeDtypeStruct((n_slots, hw), F32),
        compiler_params=pltpu.CompilerParams(
            dimension_semantics=("arbitrary",), vmem_limit_bytes=V7X_VMEM_LIMIT,
            has_side_effects=True),
        name="dispatch",
    )(dest, pend, padn, nvalid, b_pk)


def _expert_kernel(blk_e_ref, nvalid_ref, xs_ref, wg_ref, wu_ref, wd_ref, y_ref,
                   wg_bf, wu_bf, wd_bf):
    i = pl.program_id(0)
    nvalid = nvalid_ref[0]
    new_expert = jnp.logical_or(i == 0, blk_e_ref[i] != blk_e_ref[jnp.maximum(i - 1, 0)])

    @pl.when(jnp.logical_and(new_expert, i < nvalid))
    def _():
        wg_bf[...] = wg_ref[0].astype(BF16)
        wu_bf[...] = wu_ref[0].astype(BF16)
        wd_bf[...] = wd_ref[0].astype(BF16)

    @pl.when(i < nvalid)
    def _():
        xb = xs_ref[...].astype(BF16)
        hg = jnp.dot(xb, wg_bf[...], preferred_element_type=F32)
        hu = jnp.dot(xb, wu_bf[...], preferred_element_type=F32)
        h = (hg * jax.nn.sigmoid(hg) * hu).astype(BF16)
        y_ref[...] = jnp.dot(h, wd_bf[...], preferred_element_type=F32)

    @pl.when(i >= nvalid)
    def _():
        y_ref[...] = jnp.zeros_like(y_ref)


def _experts(blk_e, nvalid, xs, w_gate, w_up, w_down, bm):
    n_slots, hw = xs.shape
    nblk = n_slots // bm
    _, d, de = w_gate.shape
    last = lambda i, nv: jnp.maximum(jnp.minimum(i, nv[0] - 1), 0)
    grid_spec = pltpu.PrefetchScalarGridSpec(
        num_scalar_prefetch=2,
        grid=(nblk,),
        in_specs=[
            pl.BlockSpec((bm, hw), lambda i, be, nv: (last(i, nv), 0)),
            pl.BlockSpec((1, d, de), lambda i, be, nv: (be[last(i, nv)], 0, 0)),
            pl.BlockSpec((1, d, de), lambda i, be, nv: (be[last(i, nv)], 0, 0)),
            pl.BlockSpec((1, de, d), lambda i, be, nv: (be[last(i, nv)], 0, 0)),
        ],
        out_specs=pl.BlockSpec((bm, hw), lambda i, be, nv: (i, 0)),
        scratch_shapes=[pltpu.VMEM((d, de), BF16), pltpu.VMEM((d, de), BF16),
                        pltpu.VMEM((de, d), BF16)],
    )
    return pl.pallas_call(
        _expert_kernel,
        grid_spec=grid_spec,
        out_shape=jax.ShapeDtypeStruct((n_slots, hw), F32),
        compiler_params=pltpu.CompilerParams(
            dimension_semantics=("arbitrary",), vmem_limit_bytes=V7X_VMEM_LIMIT),
        name="experts",
    )(blk_e, nvalid, xs, w_gate, w_up, w_down)


def _combine_kernel(dest_ref, h1_ref, rg_ref, y_hbm, out_ref, ybuf, sem):
    tm = h1_ref.shape[0]
    i = pl.program_id(0)
    n_steps = pl.num_programs(0)
    slot = i % 2

    def row_copy(tile, r, k, s):
        p = dest_ref[(tile * tm + r) * TOP_K + k]
        return pltpu.make_async_copy(y_hbm.at[pl.ds(p, 1)], ybuf.at[s, k, pl.ds(r, 1)], sem.at[s])

    def issue(tile, s):
        def body(r8, c):
            for u in range(DMA_UNROLL):
                for k in range(TOP_K):
                    row_copy(tile, r8 * DMA_UNROLL + u, k, s).start()
            return c
        lax.fori_loop(0, tm // DMA_UNROLL, body, 0)

    @pl.when(i == 0)
    def _():
        issue(0, 0)

    @pl.when(i + 1 < n_steps)
    def _():
        issue(i + 1, 1 - slot)

    def drain(r8, c):
        for u in range(DMA_UNROLL):
            for k in range(TOP_K):
                row_copy(i, r8 * DMA_UNROLL + u, k, slot).wait()
        return c
    lax.fori_loop(0, tm // DMA_UNROLL, drain, 0)

    rg = rg_ref[...]
    out = h1_ref[...]
    for k in range(TOP_K):
        out = out + rg[:, k:k + 1] * ybuf[slot, k]
    out_ref[...] = out


def _combine(dest, h1, rg, y, tm):
    n, d = h1.shape
    hw = y.shape[1]
    row = lambda i, de: (i, 0)
    grid_spec = pltpu.PrefetchScalarGridSpec(
        num_scalar_prefetch=1,
        grid=(n // tm,),
        in_specs=[
            pl.BlockSpec((tm, d), row),
            pl.BlockSpec((tm, ROUTER_LANES), row),
            pl.BlockSpec(memory_space=pl.ANY),
        ],
        out_specs=pl.BlockSpec((tm, d), row),
        scratch_shapes=[pltpu.VMEM((2, TOP_K, tm, hw), F32), pltpu.SemaphoreType.DMA((2,))],
    )
    return pl.pallas_call(
        _combine_kernel,
        grid_spec=grid_spec,
        out_shape=jax.ShapeDtypeStruct((n, d), F32),
        compiler_params=pltpu.CompilerParams(
            dimension_semantics=("arbitrary",), vmem_limit_bytes=V7X_VMEM_LIMIT),
        name="combine",
    )(dest, h1, rg, y)


def _rope_tables(seq, tm):
    rows = seq // GRID_W
    inv_freq = jnp.power(ROPE_THETA, -jnp.arange(0, ROPE_AXIS_DIM, 2, dtype=F32) / ROPE_AXIS_DIM)
    ang_r = jnp.arange(rows).astype(F32)[:, None] * inv_freq[None, :]
    ang_c = jnp.arange(GRID_W).astype(F32)[:, None] * inv_freq[None, :]
    zr = jnp.zeros((rows, ROPE_AXIS_DIM), F32)
    zc = jnp.zeros((GRID_W, ROPE_AXIS_DIM), F32)
    cr, sr, cc, sc = jnp.cos(ang_r), jnp.sin(ang_r), jnp.cos(ang_c), jnp.sin(ang_c)
    per_tile = lambda a: a.reshape(seq // tm, tm // GRID_W, HEAD_DIM)
    crow = per_tile(jnp.concatenate([cr, cr, zr], axis=-1))
    srow = per_tile(jnp.concatenate([-sr, sr, zr], axis=-1))
    ccol = jnp.concatenate([zc, cc, cc], axis=-1)
    scol = jnp.concatenate([zc, -sc, sc], axis=-1)
    return crow, srow, ccol, scol


def _block_plan(counts, nblk, bm):
    padded = (counts + bm - 1) // bm * bm
    pends = jnp.cumsum(padded)
    blk_start = jnp.arange(nblk, dtype=jnp.int32) * bm
    blk_e = jnp.minimum(jnp.sum(pends[None, :] <= blk_start[:, None], axis=1), N_EXPERTS - 1)
    nvalid = (pends[-1] // bm).reshape(1)
    padn = padded - counts
    return (blk_e.astype(jnp.int32), nvalid.astype(jnp.int32), pends.astype(jnp.int32),
            padn.astype(jnp.int32))


def kernel(x, meta_tokens, norm1_g, w_in, q_norm_g, k_norm_g, pool_w, pool_scale, w_out, norm2_g,
           w_router_group, b_router_group, w_router_expert, b_router_expert, w_gate, w_up, w_down):
    batch, seq, d = x.shape
    n = batch * seq
    assert norm1_g.shape[0] == 1, "single-layer block"
    assert seq % GRID_W == 0 and seq % 256 == 0

    tm_in = _tile(seq // 2, 512)
    tm_post = _tile(seq, 512)
    tq = _tile(seq, 256)
    bm = 256
    tm_moe = _tile(seq, 512)

    x2d = x.reshape(n, d)
    w_in_bf = w_in[0].astype(BF16)
    tables = _rope_tables(seq, tm_in)
    qg = q_norm_g[0].reshape(1, HEAD_DIM)
    kg = k_norm_g[0].reshape(1, HEAD_DIM)
    g1 = norm1_g[0].reshape(1, d)

    qt, k, vt, pp = _inproj(x2d, g1, w_in_bf, tables, qg, kg, batch, seq, tm_in, 2, True)
    _, km, vmt, pm = _inproj(meta_tokens.astype(F32), g1, w_in_bf, tables, qg, kg, 1, N_META, N_META,
                             1, False)

    attn = _attention(qt, k, vt, km[0], vmt[0, :, 0], tq)

    wr = jnp.concatenate([w_router_expert[0], w_router_group[0],
                          jnp.zeros((d, ROUTER_LANES - N_EXPERTS - N_EXPERT_GROUPS), F32)], axis=1)
    br = jnp.concatenate([b_router_expert[0], b_router_group[0],
                          jnp.zeros((ROUTER_LANES - N_EXPERTS - N_EXPERT_GROUPS,), F32)])
    h1, b_pk, ri, rg, cnt = _post(attn.reshape(n, ATTN_WIDTH), pp, pm, x2d, pool_w[0].astype(BF16),
                             pool_scale[0].reshape(1, POOL_WIDTH), w_out[0].astype(BF16),
                             norm2_g[0].reshape(1, d), wr.astype(BF16),
                             br.reshape(1, ROUTER_LANES), seq, tm_post, 2)

    nblk = -(-(n * TOP_K) // bm) + N_EXPERTS
    dest = _rank(ri, cnt, bm, _tile(seq, 1024))[:, :TOP_K].reshape(-1)
    blk_e, nvalid, pends, padn = _block_plan(cnt[0, :N_EXPERTS].astype(jnp.int32), nblk, bm)
    xs = _dispatch(dest, pends, padn, nvalid, b_pk, nblk * bm, tm_moe, bm)
    y = _experts(blk_e, nvalid, xs, w_gate[0], w_up[0], w_down[0], bm)
    out = _combine(dest, h1, rg, y, tm_moe)
    return out.reshape(batch, seq, d)
```

```python
import functools
import math

import jax
import jax.numpy as jnp
from jax import lax
from jax.experimental import pallas as pl
from jax.experimental.pallas import tpu as pltpu

N_META = 16
GRID_W = 64
HEAD_DIM = 128
N_Q_HEADS = 8
N_KV_HEADS = 2
Q_PER_KV = N_Q_HEADS // N_KV_HEADS
ATTN_WIDTH = N_Q_HEADS * HEAD_DIM
KV_WIDTH = N_KV_HEADS * HEAD_DIM
POOL_WINDOWS = (2, 4, 8, 16)
N_POOL_GROUPS = len(POOL_WINDOWS)
POOL_GROUP = 256
POOL_WIDTH = N_POOL_GROUPS * POOL_GROUP
ROPE_THETA = 10000.0
ROPE_AXIS_DIM = HEAD_DIM // 2
N_EXPERT_GROUPS = 4
EXPERTS_PER_GROUP = 8
N_EXPERTS = N_EXPERT_GROUPS * EXPERTS_PER_GROUP
TOP_K = 2
EPS = 1e-6

V_ONES_ROWS = 16
V_ROWS = HEAD_DIM + V_ONES_ROWS
F32_SUBLANES = 8
HALO = F32_SUBLANES
ROUTER_LANES = 128
GROUP_LANE0 = N_EXPERTS
DMA_UNROLL = 16
LOG2E = math.log2(math.e)
Q_SCALE = HEAD_DIM ** -0.5 * LOG2E

V7X_VMEM_LIMIT = 56 * 1024 * 1024

F32 = jnp.float32
BF16 = jnp.bfloat16


def _tile(n, pref):
    t = min(n, pref)
    while n % t:
        t //= 2
    return t


def _inproj_kernel(x_ref, g1_ref, w_ref, crow_ref, srow_ref, ccol_ref, scol_ref, qg_ref, kg_ref,
                   qt_ref, k_ref, vt_ref, p_ref, *, nsub, rope):
    tm = x_ref.shape[0]
    ts = tm // nsub
    lane = lax.broadcasted_iota(jnp.int32, (ts, HEAD_DIM), 1)
    first_half = (lane % ROPE_AXIS_DIM) < (ROPE_AXIS_DIM // 2)
    qg = qg_ref[...]
    kg = kg_ref[...]

    for s in range(nsub):
        rows = slice(s * ts, (s + 1) * ts)
        xf = x_ref[rows, :]
        ms = jnp.mean(xf * xf, axis=-1, keepdims=True)
        a = (xf * lax.rsqrt(ms + EPS) * g1_ref[...]).astype(BF16)
        if rope:
            g0 = s * ts // GRID_W
            cos = jnp.concatenate([crow_ref[0, g:g + 1, :] + ccol_ref[...]
                                   for g in range(g0, g0 + ts // GRID_W)], axis=0)
            sin = jnp.concatenate([srow_ref[0, g:g + 1, :] + scol_ref[...]
                                   for g in range(g0, g0 + ts // GRID_W)], axis=0)

        def norm_rope(hd, g, scale):
            n = hd * lax.rsqrt(jnp.mean(hd * hd, axis=-1, keepdims=True) + EPS) * g
            if not rope:
                return n * scale
            partner = jnp.where(first_half,
                                pltpu.roll(n, HEAD_DIM - ROPE_AXIS_DIM // 2, 1),
                                pltpu.roll(n, ROPE_AXIS_DIM // 2, 1))
            return (n * cos + partner * sin) * scale

        for c in range(N_Q_HEADS // 2):
            pr = jnp.dot(a, w_ref[:, c * 256:(c + 1) * 256], preferred_element_type=F32)
            for j in range(2):
                qh = norm_rope(pr[:, j * HEAD_DIM:(j + 1) * HEAD_DIM], qg, Q_SCALE)
                qt_ref[0, 2 * c + j, :, rows] = qh.T.astype(BF16)
        pr = jnp.dot(a, w_ref[:, ATTN_WIDTH:ATTN_WIDTH + KV_WIDTH], preferred_element_type=F32)
        for j in range(N_KV_HEADS):
            k_ref[0, j, rows, :] = norm_rope(pr[:, j * HEAD_DIM:(j + 1) * HEAD_DIM], kg, 1.0).astype(BF16)
        pr = jnp.dot(a, w_ref[:, ATTN_WIDTH + KV_WIDTH:ATTN_WIDTH + 2 * KV_WIDTH],
                     preferred_element_type=F32)
        for j in range(N_KV_HEADS):
            vt_ref[0, j, 0, 0:HEAD_DIM, rows] = pr[:, j * HEAD_DIM:(j + 1) * HEAD_DIM].T.astype(BF16)
            vt_ref[0, j, 0, HEAD_DIM:V_ROWS, rows] = jnp.ones((V_ONES_ROWS, ts), BF16)
        p_ref[rows, :] = jnp.dot(a, w_ref[:, ATTN_WIDTH + 2 * KV_WIDTH:], preferred_element_type=F32)


def _inproj(x2d, g1, w_bf, tables, qg, kg, batch, seq, tm, nsub, rope):
    n, d = x2d.shape
    nj = seq // tm
    in_w = w_bf.shape[1]
    crow, srow, ccol, scol = tables
    gr = crow.shape[1]
    const = lambda i: (0, 0)
    return pl.pallas_call(
        functools.partial(_inproj_kernel, nsub=nsub, rope=rope),
        grid=(n // tm,),
        in_specs=[
            pl.BlockSpec((tm, d), lambda i: (i, 0)),
            pl.BlockSpec((1, d), const),
            pl.BlockSpec((d, in_w), const),
            pl.BlockSpec((1, gr, HEAD_DIM), lambda i: (i % nj, 0, 0)),
            pl.BlockSpec((1, gr, HEAD_DIM), lambda i: (i % nj, 0, 0)),
            pl.BlockSpec((GRID_W, HEAD_DIM), const),
            pl.BlockSpec((GRID_W, HEAD_DIM), const),
            pl.BlockSpec((1, HEAD_DIM), const),
            pl.BlockSpec((1, HEAD_DIM), const),
        ],
        out_specs=[
            pl.BlockSpec((1, N_Q_HEADS, HEAD_DIM, tm), lambda i: (i // nj, 0, 0, i % nj)),
            pl.BlockSpec((1, N_KV_HEADS, tm, HEAD_DIM), lambda i: (i // nj, 0, i % nj, 0)),
            pl.BlockSpec((1, N_KV_HEADS, 1, V_ROWS, tm), lambda i: (i // nj, 0, i % nj, 0, 0)),
            pl.BlockSpec((tm, POOL_WIDTH), lambda i: (i, 0)),
        ],
        out_shape=[
            jax.ShapeDtypeStruct((batch, N_Q_HEADS, HEAD_DIM, seq), BF16),
            jax.ShapeDtypeStruct((batch, N_KV_HEADS, seq, HEAD_DIM), BF16),
            jax.ShapeDtypeStruct((batch, N_KV_HEADS, nj, V_ROWS, tm), BF16),
            jax.ShapeDtypeStruct((n, POOL_WIDTH), F32),
        ],
        compiler_params=pltpu.CompilerParams(
            dimension_semantics=("arbitrary",), vmem_limit_bytes=V7X_VMEM_LIMIT),
        name="inproj",
    )(x2d, g1, w_bf, crow, srow, ccol, scol, qg, kg)


def _attn_kernel(qt_ref, k_ref, vt_ref, km_ref, vmt_ref, o_ref, s_scr, acc_scr):
    tq = qt_ref.shape[3]
    nk = vt_ref.shape[2]
    tk = vt_ref.shape[4]
    unroll = next(u for u in (8, 4, 2) if nk % u == 0)
    qt = jnp.concatenate([qt_ref[0, g] for g in range(Q_PER_KV)], axis=1)

    def scores(j):
        off = pl.multiple_of(j * tk, tk)
        return jnp.dot(k_ref[0, 0, pl.ds(off, tk), :], qt, preferred_element_type=F32)

    s0 = jnp.dot(km_ref[0], qt, preferred_element_type=F32)
    m0 = jnp.max(s0, axis=0, keepdims=True)
    p0 = jnp.exp2(s0 - m0).astype(BF16)
    acc_scr[...] = jnp.dot(vmt_ref[0], p0, preferred_element_type=F32)
    s = scores(0)
    s_scr[0] = s
    mt0 = jnp.max(s, axis=0, keepdims=True)

    def step(j, cur, m, mt, lookahead=True):
        if lookahead:
            s_next = scores(j + 1)
            s_scr[1 - cur] = s_next
            mt_next = jnp.max(s_next, axis=0, keepdims=True)
        else:
            mt_next = mt
        m_new = jnp.maximum(m, mt)
        alpha = jnp.exp2(m - m_new)
        p = jnp.exp2(s_scr[cur] - m_new).astype(BF16)
        pv = jnp.dot(vt_ref[0, 0, j], p, preferred_element_type=F32)
        acc_scr[...] = alpha * acc_scr[...] + pv
        return m_new, mt_next

    def body(jb, carry):
        m, mt = carry
        for u in range(unroll):
            m, mt = step(unroll * jb + u, u % 2, m, mt)
        return m, mt

    m, mt = lax.fori_loop(0, nk // unroll - 1, body, (m0, mt0))
    for u in range(unroll):
        m, mt = step(nk - unroll + u, u % 2, m, mt, lookahead=u < unroll - 1)
    out_t = acc_scr[0:HEAD_DIM, :] / acc_scr[HEAD_DIM:HEAD_DIM + 1, :]
    for g in range(Q_PER_KV):
        o_ref[0, :, g * HEAD_DIM:(g + 1) * HEAD_DIM] = out_t[:, g * tq:(g + 1) * tq].T.astype(BF16)


def _attention(qt, k, vt, km, vmt, tq):
    batch, _, _, seq = qt.shape
    nk, tk = vt.shape[2], vt.shape[4]
    assert nk % 2 == 0
    gw = Q_PER_KV * HEAD_DIM
    return pl.pallas_call(
        _attn_kernel,
        grid=(batch, N_KV_HEADS, seq // tq),
        in_specs=[
            pl.BlockSpec((1, Q_PER_KV, HEAD_DIM, tq), lambda b, h, i: (b, h, 0, i)),
            pl.BlockSpec((1, 1, seq, HEAD_DIM), lambda b, h, i: (b, h, 0, 0)),
            pl.BlockSpec((1, 1, nk, V_ROWS, tk), lambda b, h, i: (b, h, 0, 0, 0)),
            pl.BlockSpec((1, N_META, HEAD_DIM), lambda b, h, i: (h, 0, 0)),
            pl.BlockSpec((1, V_ROWS, N_META), lambda b, h, i: (h, 0, 0)),
        ],
        out_specs=pl.BlockSpec((1, tq, gw), lambda b, h, i: (b, i, h)),
        out_shape=jax.ShapeDtypeStruct((batch, seq, ATTN_WIDTH), BF16),
        scratch_shapes=[pltpu.VMEM((2, tk, Q_PER_KV * tq), F32),
                        pltpu.VMEM((V_ROWS, Q_PER_KV * tq), F32)],
        compiler_params=pltpu.CompilerParams(
            dimension_semantics=("arbitrary", "arbitrary", "arbitrary"),
            vmem_limit_bytes=V7X_VMEM_LIMIT),
        name="attention",
    )(qt, k, vt, km, vmt)


def _post_kernel(attn_ref, pp_ref, prev_ref, next_ref, pm_ref, x_ref, pw_ref, ps_ref, wo_ref,
                 g2_ref, wr_ref, br_ref, h1_ref, b_ref, ri_ref, rg_ref, cnt_ref, ext_ref, mix_ref,
                 *, nj, seq, nsub):
    tm = x_ref.shape[0]
    ts = tm // nsub
    j = pl.program_id(0) % nj
    ext_ref[0:HALO] = jnp.where(j == 0, pm_ref[N_META - HALO:N_META, :], prev_ref[...])
    ext_ref[HALO:HALO + tm] = pp_ref[...]
    ext_ref[HALO + tm:2 * HALO + tm] = jnp.where(j == nj - 1, 0.0, next_ref[...])

    @pl.when(pl.program_id(0) == 0)
    def _():
        cnt_ref[...] = jnp.zeros_like(cnt_ref)

    lane = lax.broadcasted_iota(jnp.int32, (ts, ROUTER_LANES), 1)
    lane_f = lane.astype(F32)
    neg = jnp.float32(-jnp.inf)
    big = jnp.float32(ROUTER_LANES)
    is_group = (lane >= GROUP_LANE0) & (lane < GROUP_LANE0 + N_EXPERT_GROUPS)

    for s in range(nsub):
        r0 = s * ts
        rows = slice(r0, r0 + ts)
        mix_ref[rows, 0:ATTN_WIDTH] = attn_ref[rows, :]
        for g, win in enumerate(POOL_WINDOWS):
            c0, c1 = g * POOL_GROUP, (g + 1) * POOL_GROUP
            half = win // 2
            ext_rows = ts + 2 * HALO
            e = ext_ref[r0:r0 + ext_rows, c0:c1]
            ahead = lambda a, k: pltpu.roll(a, ext_rows - k, 0)
            behind = lambda a, k: pltpu.roll(a, k, 0)
            fwd = e
            k = 1
            while k < half:
                fwd = fwd + ahead(fwd, k)
                k *= 2
            tot = (fwd + behind(fwd, half))[HALO:HALO + ts]
            if s == nsub - 1:
                t8 = j * tm + (tm - HALO) + lax.broadcasted_iota(jnp.int32, (HALO, 1), 0)
                cnt8 = (jnp.minimum(t8 + half, seq) - (t8 - half)).astype(F32)
                mean = jnp.concatenate([tot[0:ts - HALO] * (1.0 / win), tot[ts - HALO:ts] / cnt8], axis=0)
            else:
                mean = tot * (1.0 / win)
            mg = (mean - pp_ref[rows, c0:c1]).astype(BF16)
            yg = jnp.dot(mg, pw_ref[g], preferred_element_type=F32) * ps_ref[:, c0:c1]
            mix_ref[rows, ATTN_WIDTH + c0:ATTN_WIDTH + c1] = yg.astype(BF16)

        h1 = x_ref[rows, :] + jnp.dot(mix_ref[rows, :], wo_ref[...], preferred_element_type=F32)
        h1_ref[rows, :] = h1
        ms = jnp.mean(h1 * h1, axis=-1, keepdims=True)
        bf = h1 * lax.rsqrt(ms + EPS) * g2_ref[...]
        b_ref[rows, :] = bf
        logits = jnp.dot(bf.astype(BF16), wr_ref[...], preferred_element_type=F32) + br_ref[...]

        gl = jnp.where(is_group, logits, neg)
        gmax = jnp.max(gl, axis=1, keepdims=True)
        gidx = jnp.min(jnp.where(gl == gmax, lane_f, big), axis=1, keepdims=True) - GROUP_LANE0
        g_p = 1.0 / jnp.sum(jnp.exp(gl - gmax), axis=1, keepdims=True)
        in_group = (lane // EXPERTS_PER_GROUP).astype(F32) == gidx
        el = jnp.where(in_group & (lane < N_EXPERTS), logits, neg)
        m1 = jnp.max(el, axis=1, keepdims=True)
        i1 = jnp.min(jnp.where(el == m1, lane_f, big), axis=1, keepdims=True)
        el2 = jnp.where(lane_f == i1, neg, el)
        m2 = jnp.max(el2, axis=1, keepdims=True)
        i2 = jnp.min(jnp.where(el2 == m2, lane_f, big), axis=1, keepdims=True)
        p2 = jnp.exp(m2 - m1)
        den = 1.0 + p2
        gate1 = g_p / den
        gate2 = g_p * p2 / den
        ri_ref[rows, :] = jnp.where(lane == 0, i1, jnp.where(lane == 1, i2, 0.0)).astype(jnp.int32)
        rg_ref[rows, :] = jnp.where(lane == 0, gate1, jnp.where(lane == 1, gate2, 0.0))

        chosen = jnp.where(lane_f == i1, 1.0, 0.0) + jnp.where(lane_f == i2, 1.0, 0.0)
        cnt_ref[...] += jnp.broadcast_to(jnp.sum(chosen, axis=0, keepdims=True), cnt_ref.shape)


def _post(attn2d, pp, pm, x2d, pw_bf, ps, wo_bf, g2, wr_bf, br, seq, tm, nsub):
    n, d = x2d.shape
    nj = seq // tm
    hb = tm // HALO
    last_hb = n // HALO - 1
    const2 = lambda i: (0, 0)
    row = lambda i: (i, 0)
    return pl.pallas_call(
        functools.partial(_post_kernel, nj=nj, seq=seq, nsub=nsub),
        grid=(n // tm,),
        in_specs=[
            pl.BlockSpec((tm, ATTN_WIDTH), row),
            pl.BlockSpec((tm, POOL_WIDTH), row),
            pl.BlockSpec((HALO, POOL_WIDTH), lambda i: (jnp.maximum(i * hb - 1, 0), 0)),
            pl.BlockSpec((HALO, POOL_WIDTH), lambda i: (jnp.minimum((i + 1) * hb, last_hb), 0)),
            pl.BlockSpec((N_META, POOL_WIDTH), const2),
            pl.BlockSpec((tm, d), row),
            pl.BlockSpec((N_POOL_GROUPS, POOL_GROUP, POOL_GROUP), lambda i: (0, 0, 0)),
            pl.BlockSpec((1, POOL_WIDTH), const2),
            pl.BlockSpec((ATTN_WIDTH + POOL_WIDTH, d), const2),
            pl.BlockSpec((1, d), const2),
            pl.BlockSpec((d, ROUTER_LANES), const2),
            pl.BlockSpec((1, ROUTER_LANES), const2),
        ],
        out_specs=[
            pl.BlockSpec((tm, d), row),
            pl.BlockSpec((tm, d), row),
            pl.BlockSpec((tm, ROUTER_LANES), row),
            pl.BlockSpec((tm, ROUTER_LANES), row),
            pl.BlockSpec((8, ROUTER_LANES), const2),
        ],
        out_shape=[
            jax.ShapeDtypeStruct((n, d), F32),
            jax.ShapeDtypeStruct((n, d), F32),
            jax.ShapeDtypeStruct((n, ROUTER_LANES), jnp.int32),
            jax.ShapeDtypeStruct((n, ROUTER_LANES), F32),
            jax.ShapeDtypeStruct((8, ROUTER_LANES), F32),
        ],
        scratch_shapes=[pltpu.VMEM((tm + 2 * HALO, POOL_WIDTH), F32),
                        pltpu.VMEM((tm, ATTN_WIDTH + POOL_WIDTH), BF16)],
        compiler_params=pltpu.CompilerParams(
            dimension_semantics=("arbitrary",), vmem_limit_bytes=V7X_VMEM_LIMIT),
        name="post",
    )(attn2d, pp, pp, pp, pm, x2d, pw_bf, ps, wo_bf, g2, wr_bf, br)


def _rank_kernel(ri_ref, cnt_ref, lower_ref, dest_ref, carry, pstart, *, bm):
    t = ri_ref.shape[0]

    @pl.when(pl.program_id(0) == 0)
    def _():
        counts = cnt_ref[...]
        padded = jnp.floor((counts + (bm - 1)) * (1.0 / bm)) * bm
        lane8 = lax.broadcasted_iota(jnp.int32, (8, ROUTER_LANES), 1)
        incl = padded
        shift = 1
        while shift < ROUTER_LANES:
            incl = incl + jnp.where(lane8 >= shift, pltpu.roll(incl, shift, 1), 0.0)
            shift *= 2
        pstart[...] = (incl - padded)[0:1]
        carry[...] = jnp.zeros_like(carry)

    lane = lax.broadcasted_iota(jnp.int32, (t, ROUTER_LANES), 1)
    ri = ri_ref[...]
    oh0 = lane == ri[:, 0:1]
    oh1 = lane == ri[:, 1:2]
    ohf = jnp.where(oh0, 1.0, 0.0) + jnp.where(oh1, 1.0, 0.0)
    prefix = jnp.dot(lower_ref[...], ohf.astype(BF16), preferred_element_type=F32)
    base = prefix + carry[...] + pstart[...]
    d0 = jnp.sum(jnp.where(oh0, base, 0.0), axis=1, keepdims=True)
    d1 = jnp.sum(jnp.where(oh1, base, 0.0), axis=1, keepdims=True)
    dest_ref[...] = jnp.where(lane == 0, d0, jnp.where(lane == 1, d1, 0.0)).astype(jnp.int32)
    carry[...] += jnp.sum(ohf, axis=0, keepdims=True)


def _rank(ri, cnt, bm, t):
    n = ri.shape[0]
    lower = jnp.tril(jnp.ones((t, t), BF16), -1)
    return pl.pallas_call(
        functools.partial(_rank_kernel, bm=bm),
        grid=(n // t,),
        in_specs=[
            pl.BlockSpec((t, ROUTER_LANES), lambda i: (i, 0)),
            pl.BlockSpec((8, ROUTER_LANES), lambda i: (0, 0)),
            pl.BlockSpec((t, t), lambda i: (0, 0)),
        ],
        out_specs=pl.BlockSpec((t, ROUTER_LANES), lambda i: (i, 0)),
        out_shape=jax.ShapeDtypeStruct((n, ROUTER_LANES), jnp.int32),
        scratch_shapes=[pltpu.VMEM((1, ROUTER_LANES), F32), pltpu.VMEM((1, ROUTER_LANES), F32)],
        compiler_params=pltpu.CompilerParams(
            dimension_semantics=("arbitrary",), vmem_limit_bytes=V7X_VMEM_LIMIT),
        name="rank",
    )(ri, cnt, lower)


def _dispatch_kernel(dest_ref, pend_ref, padn_ref, nvalid_ref, b_ref, xs_ref, zeros, sem, zsem, *, bm):
    tm = b_ref.shape[0]
    i = pl.program_id(0)

    @pl.when(i == 0)
    def _():
        zeros[...] = jnp.zeros_like(zeros)
        nblk = xs_ref.shape[0] // bm

        def pad_copy(e, g):
            start = pl.multiple_of(pend_ref[e] - (g + 1) * F32_SUBLANES, F32_SUBLANES)
            return pltpu.make_async_copy(zeros.at[pl.ds(0, F32_SUBLANES)], xs_ref.at[pl.ds(start, F32_SUBLANES)], zsem)

        def tail_copy(blk):
            return pltpu.make_async_copy(zeros, xs_ref.at[pl.ds(pl.multiple_of(blk * bm, bm), bm)], zsem)

        def n_groups(e):
            return (padn_ref[e] + (F32_SUBLANES - 1)) // F32_SUBLANES

        def per_expert(e, c):
            lax.fori_loop(0, n_groups(e), lambda g, c2: (pad_copy(e, g).start(), c2)[1], 0)
            return c
        lax.fori_loop(0, N_EXPERTS, per_expert, 0)
        lax.fori_loop(nvalid_ref[0], nblk, lambda blk, c: (tail_copy(blk).start(), c)[1], 0)

        def per_expert_wait(e, c):
            lax.fori_loop(0, n_groups(e), lambda g, c2: (pad_copy(e, g).wait(), c2)[1], 0)
            return c
        lax.fori_loop(0, N_EXPERTS, per_expert_wait, 0)
        lax.fori_loop(nvalid_ref[0], nblk, lambda blk, c: (tail_copy(blk).wait(), c)[1], 0)

    def row_copy(r, k):
        p = dest_ref[(i * tm + r) * TOP_K + k]
        return pltpu.make_async_copy(b_ref.at[pl.ds(r, 1)], xs_ref.at[pl.ds(p, 1)], sem)

    def issue(r8, c):
        for u in range(DMA_UNROLL):
            for k in range(TOP_K):
                row_copy(r8 * DMA_UNROLL + u, k).start()
        return c
    lax.fori_loop(0, tm // DMA_UNROLL, issue, 0)

    def drain(r8, c):
        for u in range(DMA_UNROLL):
            for k in range(TOP_K):
                row_copy(r8 * DMA_UNROLL + u, k).wait()
        return c
    lax.fori_loop(0, tm // DMA_UNROLL, drain, 0)


def _dispatch(dest, pend, padn, nvalid, b_pk, n_slots, tm, bm):
    n, hw = b_pk.shape
    grid_spec = pltpu.PrefetchScalarGridSpec(
        num_scalar_prefetch=4,
        grid=(n // tm,),
        in_specs=[pl.BlockSpec((tm, hw), lambda i, *_: (i, 0))],
        out_specs=pl.BlockSpec(memory_space=pl.ANY),
        scratch_shapes=[pltpu.VMEM((bm, hw), F32), pltpu.SemaphoreType.DMA(()),
                        pltpu.SemaphoreType.DMA(())],
    )
    return pl.pallas_call(
        functools.partial(_dispatch_kernel, bm=bm),
        grid_spec=grid_spec,
        out_shape=jax.ShapeDtypeStruct((n_slots, hw), F32),
        compiler_params=pltpu.CompilerParams(
            dimension_semantics=("arbitrary",), vmem_limit_bytes=V7X_VMEM_LIMIT,
            has_side_effects=True),
        name="dispatch",
    )(dest, pend, padn, nvalid, b_pk)


def _expert_kernel(blk_e_ref, nvalid_ref, eslot_ref, nxt_e_ref, xs_ref, wg_hbm, wu_hbm, wd_hbm, y_ref,
                   wg_f32, wu_f32, wd_f32, wg_bf, wu_bf, wd_bf, sem):
    i = pl.program_id(0)
    nvalid = nvalid_ref[0]
    e = blk_e_ref[i]
    slot = eslot_ref[i]
    new_expert = jnp.logical_or(i == 0, e != blk_e_ref[jnp.maximum(i - 1, 0)])

    def fetch(expert, s):
        return (pltpu.make_async_copy(wg_hbm.at[expert], wg_f32.at[s], sem.at[0, s]),
                pltpu.make_async_copy(wu_hbm.at[expert], wu_f32.at[s], sem.at[1, s]),
                pltpu.make_async_copy(wd_hbm.at[expert], wd_f32.at[s], sem.at[2, s]))

    @pl.when(jnp.logical_and(i == 0, nvalid > 0))
    def _():
        for c in fetch(e, slot):
            c.start()

    @pl.when(jnp.logical_and(new_expert, i < nvalid))
    def _():
        for c in fetch(e, slot):
            c.wait()
        for static_slot in range(2):
            @pl.when(slot == static_slot)
            def _(static_slot=static_slot):
                wg_bf[...] = wg_f32[static_slot].astype(BF16)
                wu_bf[...] = wu_f32[static_slot].astype(BF16)
                wd_bf[...] = wd_f32[static_slot].astype(BF16)

        @pl.when(nxt_e_ref[i] >= 0)
        def _():
            for c in fetch(nxt_e_ref[i], 1 - slot):
                c.start()

    @pl.when(i < nvalid)
    def _():
        xb = xs_ref[...].astype(BF16)
        hg = jnp.dot(xb, wg_bf[...], preferred_element_type=F32)
        hu = jnp.dot(xb, wu_bf[...], preferred_element_type=F32)
        h = (hg * jax.nn.sigmoid(hg) * hu).astype(BF16)
        y_ref[...] = jnp.dot(h, wd_bf[...], preferred_element_type=F32)

    @pl.when(i >= nvalid)
    def _():
        y_ref[...] = jnp.zeros_like(y_ref)


def _experts(blk_e, nvalid, eslot, nxt_e, xs, w_gate, w_up, w_down, bm):
    n_slots, hw = xs.shape
    nblk = n_slots // bm
    _, d, de = w_gate.shape
    last = lambda i, nv: jnp.maximum(jnp.minimum(i, nv[0] - 1), 0)
    grid_spec = pltpu.PrefetchScalarGridSpec(
        num_scalar_prefetch=4,
        grid=(nblk,),
        in_specs=[
            pl.BlockSpec((bm, hw), lambda i, be, nv, es, ne: (last(i, nv), 0)),
            pl.BlockSpec(memory_space=pl.ANY),
            pl.BlockSpec(memory_space=pl.ANY),
            pl.BlockSpec(memory_space=pl.ANY),
        ],
        out_specs=pl.BlockSpec((bm, hw), lambda i, be, nv, es, ne: (i, 0)),
        scratch_shapes=[pltpu.VMEM((2, d, de), F32), pltpu.VMEM((2, d, de), F32),
                        pltpu.VMEM((2, de, d), F32),
                        pltpu.VMEM((d, de), BF16), pltpu.VMEM((d, de), BF16),
                        pltpu.VMEM((de, d), BF16), pltpu.SemaphoreType.DMA((3, 2))],
    )
    return pl.pallas_call(
        _expert_kernel,
        grid_spec=grid_spec,
        out_shape=jax.ShapeDtypeStruct((n_slots, hw), F32),
        compiler_params=pltpu.CompilerParams(
            dimension_semantics=("arbitrary",), vmem_limit_bytes=V7X_VMEM_LIMIT),
        name="experts",
    )(blk_e, nvalid, eslot, nxt_e, xs, w_gate, w_up, w_down)


def _combine_kernel(dest_ref, h1_ref, rg_ref, y_hbm, out_ref, ybuf, sem):
    tm = h1_ref.shape[0]
    i = pl.program_id(0)
    n_steps = pl.num_programs(0)
    slot = i % 2

    def row_copy(tile, r, k, s):
        p = dest_ref[(tile * tm + r) * TOP_K + k]
        return pltpu.make_async_copy(y_hbm.at[pl.ds(p, 1)], ybuf.at[s, k, pl.ds(r, 1)], sem.at[s])

    def issue(tile, s):
        def body(r8, c):
            for u in range(DMA_UNROLL):
                for k in range(TOP_K):
                    row_copy(tile, r8 * DMA_UNROLL + u, k, s).start()
            return c
        lax.fori_loop(0, tm // DMA_UNROLL, body, 0)

    @pl.when(i == 0)
    def _():
        issue(0, 0)

    @pl.when(i + 1 < n_steps)
    def _():
        issue(i + 1, 1 - slot)

    def drain(r8, c):
        for u in range(DMA_UNROLL):
            for k in range(TOP_K):
                row_copy(i, r8 * DMA_UNROLL + u, k, slot).wait()
        return c
    lax.fori_loop(0, tm // DMA_UNROLL, drain, 0)

    rg = rg_ref[...]
    out = h1_ref[...]
    for k in range(TOP_K):
        out = out + rg[:, k:k + 1] * ybuf[slot, k]
    out_ref[...] = out


def _combine(dest, h1, rg, y, tm):
    n, d = h1.shape
    hw = y.shape[1]
    row = lambda i, de: (i, 0)
    grid_spec = pltpu.PrefetchScalarGridSpec(
        num_scalar_prefetch=1,
        grid=(n // tm,),
        in_specs=[
            pl.BlockSpec((tm, d), row),
            pl.BlockSpec((tm, ROUTER_LANES), row),
            pl.BlockSpec(memory_space=pl.ANY),
        ],
        out_specs=pl.BlockSpec((tm, d), row),
        scratch_shapes=[pltpu.VMEM((2, TOP_K, tm, hw), F32), pltpu.SemaphoreType.DMA((2,))],
    )
    return pl.pallas_call(
        _combine_kernel,
        grid_spec=grid_spec,
        out_shape=jax.ShapeDtypeStruct((n, d), F32),
        compiler_params=pltpu.CompilerParams(
            dimension_semantics=("arbitrary",), vmem_limit_bytes=V7X_VMEM_LIMIT),
        name="combine",
    )(dest, h1, rg, y)


def _rope_tables(seq, tm):
    rows = seq // GRID_W
    inv_freq = jnp.power(ROPE_THETA, -jnp.arange(0, ROPE_AXIS_DIM, 2, dtype=F32) / ROPE_AXIS_DIM)
    ang_r = jnp.arange(rows).astype(F32)[:, None] * inv_freq[None, :]
    ang_c = jnp.arange(GRID_W).astype(F32)[:, None] * inv_freq[None, :]
    zr = jnp.zeros((rows, ROPE_AXIS_DIM), F32)
    zc = jnp.zeros((GRID_W, ROPE_AXIS_DIM), F32)
    cr, sr, cc, sc = jnp.cos(ang_r), jnp.sin(ang_r), jnp.cos(ang_c), jnp.sin(ang_c)
    per_tile = lambda a: a.reshape(seq // tm, tm // GRID_W, HEAD_DIM)
    crow = per_tile(jnp.concatenate([cr, cr, zr], axis=-1))
    srow = per_tile(jnp.concatenate([-sr, sr, zr], axis=-1))
    ccol = jnp.concatenate([zc, cc, cc], axis=-1)
    scol = jnp.concatenate([zc, -sc, sc], axis=-1)
    return crow, srow, ccol, scol


def _block_plan(counts, nblk, bm):
    padded = (counts + bm - 1) // bm * bm
    pends = jnp.cumsum(padded)
    blk_start = jnp.arange(nblk, dtype=jnp.int32) * bm
    blk_e = jnp.minimum(jnp.sum(pends[None, :] <= blk_start[:, None], axis=1), N_EXPERTS - 1)
    nvalid = (pends[-1] // bm).reshape(1)
    padn = padded - counts
    used = counts > 0
    ids = jnp.arange(N_EXPERTS, dtype=jnp.int32)
    ordinal = jnp.cumsum(used) - 1
    later_used = jnp.logical_and(used[None, :], ids[None, :] > ids[:, None])
    nxt = jnp.min(jnp.where(later_used, ids[None, :], N_EXPERTS), axis=1)
    nxt = jnp.where(nxt == N_EXPERTS, -1, nxt)
    i32 = lambda a: a.astype(jnp.int32)
    return (i32(blk_e), i32(nvalid), i32(ordinal % 2)[blk_e], i32(nxt)[blk_e], i32(pends), i32(padn))


def kernel(x, meta_tokens, norm1_g, w_in, q_norm_g, k_norm_g, pool_w, pool_scale, w_out, norm2_g,
           w_router_group, b_router_group, w_router_expert, b_router_expert, w_gate, w_up, w_down):
    batch, seq, d = x.shape
    n = batch * seq
    assert norm1_g.shape[0] == 1, "single-layer block"
    assert seq % GRID_W == 0 and seq % 256 == 0

    tm_in = _tile(seq // 2, 512)
    tm_post = _tile(seq, 512)
    tq = _tile(seq, 256)
    bm = 256
    tm_moe = _tile(seq, 512)

    x2d = x.reshape(n, d)
    w_in_bf = w_in[0].astype(BF16)
    tables = _rope_tables(seq, tm_in)
    qg = q_norm_g[0].reshape(1, HEAD_DIM)
    kg = k_norm_g[0].reshape(1, HEAD_DIM)
    g1 = norm1_g[0].reshape(1, d)

    qt, k, vt, pp = _inproj(x2d, g1, w_in_bf, tables, qg, kg, batch, seq, tm_in, 2, True)
    _, km, vmt, pm = _inproj(meta_tokens.astype(F32), g1, w_in_bf, tables, qg, kg, 1, N_META, N_META,
                             1, False)

    attn = _attention(qt, k, vt, km[0], vmt[0, :, 0], tq)

    wr = jnp.concatenate([w_router_expert[0], w_router_group[0],
                          jnp.zeros((d, ROUTER_LANES - N_EXPERTS - N_EXPERT_GROUPS), F32)], axis=1)
    br = jnp.concatenate([b_router_expert[0], b_router_group[0],
                          jnp.zeros((ROUTER_LANES - N_EXPERTS - N_EXPERT_GROUPS,), F32)])
    h1, b_pk, ri, rg, cnt = _post(attn.reshape(n, ATTN_WIDTH), pp, pm, x2d, pool_w[0].astype(BF16),
                             pool_scale[0].reshape(1, POOL_WIDTH), w_out[0].astype(BF16),
                             norm2_g[0].reshape(1, d), wr.astype(BF16),
                             br.reshape(1, ROUTER_LANES), seq, tm_post, 2)

    nblk = -(-(n * TOP_K) // bm) + N_EXPERTS
    dest = _rank(ri, cnt, bm, _tile(seq, 1024))[:, :TOP_K].reshape(-1)
    blk_e, nvalid, eslot, nxt_e, pends, padn = _block_plan(cnt[0, :N_EXPERTS].astype(jnp.int32), nblk, bm)
    xs = _dispatch(dest, pends, padn, nvalid, b_pk, nblk * bm, tm_moe, bm)
    y = _experts(blk_e, nvalid, eslot, nxt_e, xs, w_gate[0], w_up[0], w_down[0], bm)
    out = _combine(dest, h1, rg, y, tm_moe)
    return out.reshape(batch, seq, d)
```

```python
import functools
import math

import jax
import jax.numpy as jnp
from jax import lax
from jax.experimental import pallas as pl
from jax.experimental.pallas import tpu as pltpu

N_META = 16
GRID_W = 64
HEAD_DIM = 128
N_Q_HEADS = 8
N_KV_HEADS = 2
Q_PER_KV = N_Q_HEADS // N_KV_HEADS
ATTN_WIDTH = N_Q_HEADS * HEAD_DIM
KV_WIDTH = N_KV_HEADS * HEAD_DIM
POOL_WINDOWS = (2, 4, 8, 16)
N_POOL_GROUPS = len(POOL_WINDOWS)
POOL_GROUP = 256
POOL_WIDTH = N_POOL_GROUPS * POOL_GROUP
ROPE_THETA = 10000.0
ROPE_AXIS_DIM = HEAD_DIM // 2
N_EXPERT_GROUPS = 4
EXPERTS_PER_GROUP = 8
N_EXPERTS = N_EXPERT_GROUPS * EXPERTS_PER_GROUP
TOP_K = 2
EPS = 1e-6

V_ONES_ROWS = 16
V_ROWS = HEAD_DIM + V_ONES_ROWS
F32_SUBLANES = 8
HALO = F32_SUBLANES
ROUTER_LANES = 128
GROUP_LANE0 = N_EXPERTS
DMA_UNROLL = 16
LOG2E = math.log2(math.e)
Q_SCALE = HEAD_DIM ** -0.5 * LOG2E

V7X_VMEM_LIMIT = 56 * 1024 * 1024

F32 = jnp.float32
BF16 = jnp.bfloat16


def _tile(n, pref):
    t = min(n, pref)
    while n % t:
        t //= 2
    return t


def _inproj_kernel(x_ref, g1_ref, w_ref, crow_ref, srow_ref, ccol_ref, scol_ref, qg_ref, kg_ref,
                   qt_ref, k_ref, vt_ref, p_ref, *, nsub, rope):
    tm = x_ref.shape[0]
    ts = tm // nsub
    lane = lax.broadcasted_iota(jnp.int32, (ts, HEAD_DIM), 1)
    first_half = (lane % ROPE_AXIS_DIM) < (ROPE_AXIS_DIM // 2)
    qg = qg_ref[...]
    kg = kg_ref[...]

    for s in range(nsub):
        rows = slice(s * ts, (s + 1) * ts)
        xf = x_ref[rows, :]
        ms = jnp.mean(xf * xf, axis=-1, keepdims=True)
        a = (xf * lax.rsqrt(ms + EPS) * g1_ref[...]).astype(BF16)
        if rope:
            g0 = s * ts // GRID_W
            cos = jnp.concatenate([crow_ref[0, g:g + 1, :] + ccol_ref[...]
                                   for g in range(g0, g0 + ts // GRID_W)], axis=0)
            sin = jnp.concatenate([srow_ref[0, g:g + 1, :] + scol_ref[...]
                                   for g in range(g0, g0 + ts // GRID_W)], axis=0)

        def norm_rope(hd, g, scale):
            n = hd * lax.rsqrt(jnp.mean(hd * hd, axis=-1, keepdims=True) + EPS) * g
            if not rope:
                return n * scale
            partner = jnp.where(first_half,
                                pltpu.roll(n, HEAD_DIM - ROPE_AXIS_DIM // 2, 1),
                                pltpu.roll(n, ROPE_AXIS_DIM // 2, 1))
            return (n * cos + partner * sin) * scale

        for c in range(N_Q_HEADS // 2):
            pr = jnp.dot(a, w_ref[:, c * 256:(c + 1) * 256], preferred_element_type=F32)
            for j in range(2):
                qh = norm_rope(pr[:, j * HEAD_DIM:(j + 1) * HEAD_DIM], qg, Q_SCALE)
                qt_ref[0, 2 * c + j, :, rows] = qh.T.astype(BF16)
        pr = jnp.dot(a, w_ref[:, ATTN_WIDTH:ATTN_WIDTH + KV_WIDTH], preferred_element_type=F32)
        for j in range(N_KV_HEADS):
            k_ref[0, j, rows, :] = norm_rope(pr[:, j * HEAD_DIM:(j + 1) * HEAD_DIM], kg, 1.0).astype(BF16)
        pr = jnp.dot(a, w_ref[:, ATTN_WIDTH + KV_WIDTH:ATTN_WIDTH + 2 * KV_WIDTH],
                     preferred_element_type=F32)
        for j in range(N_KV_HEADS):
            vt_ref[0, j, 0, 0:HEAD_DIM, rows] = pr[:, j * HEAD_DIM:(j + 1) * HEAD_DIM].T.astype(BF16)
            vt_ref[0, j, 0, HEAD_DIM:V_ROWS, rows] = jnp.ones((V_ONES_ROWS, ts), BF16)
        p_ref[rows, :] = jnp.dot(a, w_ref[:, ATTN_WIDTH + 2 * KV_WIDTH:], preferred_element_type=F32)


def _inproj(x2d, g1, w_bf, tables, qg, kg, batch, seq, tm, nsub, rope):
    n, d = x2d.shape
    nj = seq // tm
    in_w = w_bf.shape[1]
    crow, srow, ccol, scol = tables
    gr = crow.shape[1]
    const = lambda i: (0, 0)
    return pl.pallas_call(
        functools.partial(_inproj_kernel, nsub=nsub, rope=rope),
        grid=(n // tm,),
        in_specs=[
            pl.BlockSpec((tm, d), lambda i: (i, 0)),
            pl.BlockSpec((1, d), const),
            pl.BlockSpec((d, in_w), const),
            pl.BlockSpec((1, gr, HEAD_DIM), lambda i: (i % nj, 0, 0)),
            pl.BlockSpec((1, gr, HEAD_DIM), lambda i: (i % nj, 0, 0)),
            pl.BlockSpec((GRID_W, HEAD_DIM), const),
            pl.BlockSpec((GRID_W, HEAD_DIM), const),
            pl.BlockSpec((1, HEAD_DIM), const),
            pl.BlockSpec((1, HEAD_DIM), const),
        ],
        out_specs=[
            pl.BlockSpec((1, N_Q_HEADS, HEAD_DIM, tm), lambda i: (i // nj, 0, 0, i % nj)),
            pl.BlockSpec((1, N_KV_HEADS, tm, HEAD_DIM), lambda i: (i // nj, 0, i % nj, 0)),
            pl.BlockSpec((1, N_KV_HEADS, 1, V_ROWS, tm), lambda i: (i // nj, 0, i % nj, 0, 0)),
            pl.BlockSpec((tm, POOL_WIDTH), lambda i: (i, 0)),
        ],
        out_shape=[
            jax.ShapeDtypeStruct((batch, N_Q_HEADS, HEAD_DIM, seq), BF16),
            jax.ShapeDtypeStruct((batch, N_KV_HEADS, seq, HEAD_DIM), BF16),
            jax.ShapeDtypeStruct((batch, N_KV_HEADS, nj, V_ROWS, tm), BF16),
            jax.ShapeDtypeStruct((n, POOL_WIDTH), F32),
        ],
        compiler_params=pltpu.CompilerParams(
            dimension_semantics=("arbitrary",), vmem_limit_bytes=V7X_VMEM_LIMIT),
        name="inproj",
    )(x2d, g1, w_bf, crow, srow, ccol, scol, qg, kg)


def _attn_kernel(qt_ref, k_ref, vt_ref, km_ref, vmt_ref, o_ref, s_scr, acc_scr):
    tq = qt_ref.shape[3]
    nk = vt_ref.shape[2]
    tk = vt_ref.shape[4]
    unroll = next(u for u in (8, 4, 2) if nk % u == 0)
    qt = jnp.concatenate([qt_ref[0, g] for g in range(Q_PER_KV)], axis=1)

    def scores(j):
        off = pl.multiple_of(j * tk, tk)
        return jnp.dot(k_ref[0, 0, pl.ds(off, tk), :], qt, preferred_element_type=F32)

    s0 = jnp.dot(km_ref[0], qt, preferred_element_type=F32)
    m0 = jnp.max(s0, axis=0, keepdims=True)
    p0 = jnp.exp2(s0 - m0).astype(BF16)
    acc_scr[...] = jnp.dot(vmt_ref[0], p0, preferred_element_type=F32)
    s = scores(0)
    s_scr[0] = s
    mt0 = jnp.max(s, axis=0, keepdims=True)

    def step(j, cur, m, mt, lookahead=True):
        if lookahead:
            s_next = scores(j + 1)
            s_scr[1 - cur] = s_next
            mt_next = jnp.max(s_next, axis=0, keepdims=True)
        else:
            mt_next = mt
        m_new = jnp.maximum(m, mt)
        alpha = jnp.exp2(m - m_new)
        p = jnp.exp2(s_scr[cur] - m_new).astype(BF16)
        pv = jnp.dot(vt_ref[0, 0, j], p, preferred_element_type=F32)
        acc_scr[...] = alpha * acc_scr[...] + pv
        return m_new, mt_next

    def body(jb, carry):
        m, mt = carry
        for u in range(unroll):
            m, mt = step(unroll * jb + u, u % 2, m, mt)
        return m, mt

    m, mt = lax.fori_loop(0, nk // unroll - 1, body, (m0, mt0))
    for u in range(unroll):
        m, mt = step(nk - unroll + u, u % 2, m, mt, lookahead=u < unroll - 1)
    out_t = acc_scr[0:HEAD_DIM, :] / acc_scr[HEAD_DIM:HEAD_DIM + 1, :]
    for g in range(Q_PER_KV):
        o_ref[0, :, g * HEAD_DIM:(g + 1) * HEAD_DIM] = out_t[:, g * tq:(g + 1) * tq].T.astype(BF16)


def _attention(qt, k, vt, km, vmt, tq):
    batch, _, _, seq = qt.shape
    nk, tk = vt.shape[2], vt.shape[4]
    assert nk % 2 == 0
    gw = Q_PER_KV * HEAD_DIM
    return pl.pallas_call(
        _attn_kernel,
        grid=(batch, N_KV_HEADS, seq // tq),
        in_specs=[
            pl.BlockSpec((1, Q_PER_KV, HEAD_DIM, tq), lambda b, h, i: (b, h, 0, i)),
            pl.BlockSpec((1, 1, seq, HEAD_DIM), lambda b, h, i: (b, h, 0, 0)),
            pl.BlockSpec((1, 1, nk, V_ROWS, tk), lambda b, h, i: (b, h, 0, 0, 0)),
            pl.BlockSpec((1, N_META, HEAD_DIM), lambda b, h, i: (h, 0, 0)),
            pl.BlockSpec((1, V_ROWS, N_META), lambda b, h, i: (h, 0, 0)),
        ],
        out_specs=pl.BlockSpec((1, tq, gw), lambda b, h, i: (b, i, h)),
        out_shape=jax.ShapeDtypeStruct((batch, seq, ATTN_WIDTH), BF16),
        scratch_shapes=[pltpu.VMEM((2, tk, Q_PER_KV * tq), F32),
                        pltpu.VMEM((V_ROWS, Q_PER_KV * tq), F32)],
        compiler_params=pltpu.CompilerParams(
            dimension_semantics=("arbitrary", "arbitrary", "arbitrary"),
            vmem_limit_bytes=V7X_VMEM_LIMIT),
        name="attention",
    )(qt, k, vt, km, vmt)


def _post_kernel(attn_ref, pp_ref, prev_ref, next_ref, pm_ref, x_ref, pw_ref, ps_ref, wo_ref,
                 g2_ref, wr_ref, br_ref, h1_ref, b_ref, ri_ref, rg_ref, cnt_ref, ext_ref, mix_ref,
                 *, nj, seq, nsub):
    tm = x_ref.shape[0]
    ts = tm // nsub
    j = pl.program_id(0) % nj
    ext_ref[0:HALO] = jnp.where(j == 0, pm_ref[N_META - HALO:N_META, :], prev_ref[...])
    ext_ref[HALO:HALO + tm] = pp_ref[...]
    ext_ref[HALO + tm:2 * HALO + tm] = jnp.where(j == nj - 1, 0.0, next_ref[...])

    @pl.when(pl.program_id(0) == 0)
    def _():
        cnt_ref[...] = jnp.zeros_like(cnt_ref)

    lane = lax.broadcasted_iota(jnp.int32, (ts, ROUTER_LANES), 1)
    lane_f = lane.astype(F32)
    neg = jnp.float32(-jnp.inf)
    big = jnp.float32(ROUTER_LANES)
    is_group = (lane >= GROUP_LANE0) & (lane < GROUP_LANE0 + N_EXPERT_GROUPS)

    for s in range(nsub):
        r0 = s * ts
        rows = slice(r0, r0 + ts)
        mix_ref[rows, 0:ATTN_WIDTH] = attn_ref[rows, :]
        for g, win in enumerate(POOL_WINDOWS):
            c0, c1 = g * POOL_GROUP, (g + 1) * POOL_GROUP
            half = win // 2
            ext_rows = ts + 2 * HALO
            e = ext_ref[r0:r0 + ext_rows, c0:c1]
            ahead = lambda a, k: pltpu.roll(a, ext_rows - k, 0)
            behind = lambda a, k: pltpu.roll(a, k, 0)
            fwd = e
            k = 1
            while k < half:
                fwd = fwd + ahead(fwd, k)
                k *= 2
            tot = (fwd + behind(fwd, half))[HALO:HALO + ts]
            if s == nsub - 1:
                t8 = j * tm + (tm - HALO) + lax.broadcasted_iota(jnp.int32, (HALO, 1), 0)
                cnt8 = (jnp.minimum(t8 + half, seq) - (t8 - half)).astype(F32)
                mean = jnp.concatenate([tot[0:ts - HALO] * (1.0 / win), tot[ts - HALO:ts] / cnt8], axis=0)
            else:
                mean = tot * (1.0 / win)
            mg = (mean - pp_ref[rows, c0:c1]).astype(BF16)
            yg = jnp.dot(mg, pw_ref[g], preferred_element_type=F32) * ps_ref[:, c0:c1]
            mix_ref[rows, ATTN_WIDTH + c0:ATTN_WIDTH + c1] = yg.astype(BF16)

        h1 = x_ref[rows, :] + jnp.dot(mix_ref[rows, :], wo_ref[...], preferred_element_type=F32)
        h1_ref[rows, :] = h1
        ms = jnp.mean(h1 * h1, axis=-1, keepdims=True)
        bf = h1 * lax.rsqrt(ms + EPS) * g2_ref[...]
        b_ref[rows, :] = bf
        logits = jnp.dot(bf.astype(BF16), wr_ref[...], preferred_element_type=F32) + br_ref[...]

        gl = jnp.where(is_group, logits, neg)
        gmax = jnp.max(gl, axis=1, keepdims=True)
        gidx = jnp.min(jnp.where(gl == gmax, lane_f, big), axis=1, keepdims=True) - GROUP_LANE0
        g_p = 1.0 / jnp.sum(jnp.exp(gl - gmax), axis=1, keepdims=True)
        in_group = (lane // EXPERTS_PER_GROUP).astype(F32) == gidx
        el = jnp.where(in_group & (lane < N_EXPERTS), logits, neg)
        m1 = jnp.max(el, axis=1, keepdims=True)
        i1 = jnp.min(jnp.where(el == m1, lane_f, big), axis=1, keepdims=True)
        el2 = jnp.where(lane_f == i1, neg, el)
        m2 = jnp.max(el2, axis=1, keepdims=True)
        i2 = jnp.min(jnp.where(el2 == m2, lane_f, big), axis=1, keepdims=True)
        p2 = jnp.exp(m2 - m1)
        den = 1.0 + p2
        gate1 = g_p / den
        gate2 = g_p * p2 / den
        ri_ref[rows, :] = jnp.where(lane == 0, i1, jnp.where(lane == 1, i2, 0.0)).astype(jnp.int32)
        rg_ref[rows, :] = jnp.where(lane == 0, gate1, jnp.where(lane == 1, gate2, 0.0))

        chosen = jnp.where(lane_f == i1, 1.0, 0.0) + jnp.where(lane_f == i2, 1.0, 0.0)
        cnt_ref[...] += jnp.broadcast_to(jnp.sum(chosen, axis=0, keepdims=True), cnt_ref.shape)


def _post(attn2d, pp, pm, x2d, pw_bf, ps, wo_bf, g2, wr_bf, br, seq, tm, nsub):
    n, d = x2d.shape
    nj = seq // tm
    hb = tm // HALO
    last_hb = n // HALO - 1
    const2 = lambda i: (0, 0)
    row = lambda i: (i, 0)
    return pl.pallas_call(
        functools.partial(_post_kernel, nj=nj, seq=seq, nsub=nsub),
        grid=(n // tm,),
        in_specs=[
            pl.BlockSpec((tm, ATTN_WIDTH), row),
            pl.BlockSpec((tm, POOL_WIDTH), row),
            pl.BlockSpec((HALO, POOL_WIDTH), lambda i: (jnp.maximum(i * hb - 1, 0), 0)),
            pl.BlockSpec((HALO, POOL_WIDTH), lambda i: (jnp.minimum((i + 1) * hb, last_hb), 0)),
            pl.BlockSpec((N_META, POOL_WIDTH), const2),
            pl.BlockSpec((tm, d), row),
            pl.BlockSpec((N_POOL_GROUPS, POOL_GROUP, POOL_GROUP), lambda i: (0, 0, 0)),
            pl.BlockSpec((1, POOL_WIDTH), const2),
            pl.BlockSpec((ATTN_WIDTH + POOL_WIDTH, d), const2),
            pl.BlockSpec((1, d), const2),
            pl.BlockSpec((d, ROUTER_LANES), const2),
            pl.BlockSpec((1, ROUTER_LANES), const2),
        ],
        out_specs=[
            pl.BlockSpec((tm, d), row),
            pl.BlockSpec((tm, d), row),
            pl.BlockSpec((tm, ROUTER_LANES), row),
            pl.BlockSpec((tm, ROUTER_LANES), row),
            pl.BlockSpec((8, ROUTER_LANES), const2),
        ],
        out_shape=[
            jax.ShapeDtypeStruct((n, d), F32),
            jax.ShapeDtypeStruct((n, d), F32),
            jax.ShapeDtypeStruct((n, ROUTER_LANES), jnp.int32),
            jax.ShapeDtypeStruct((n, ROUTER_LANES), F32),
            jax.ShapeDtypeStruct((8, ROUTER_LANES), F32),
        ],
        scratch_shapes=[pltpu.VMEM((tm + 2 * HALO, POOL_WIDTH), F32),
                        pltpu.VMEM((tm, ATTN_WIDTH + POOL_WIDTH), BF16)],
        compiler_params=pltpu.CompilerParams(
            dimension_semantics=("arbitrary",), vmem_limit_bytes=V7X_VMEM_LIMIT),
        name="post",
    )(attn2d, pp, pp, pp, pm, x2d, pw_bf, ps, wo_bf, g2, wr_bf, br)


def _rank_kernel(ri_ref, cnt_ref, lower_ref, dest_ref, carry, pstart, *, bm):
    t = ri_ref.shape[0]

    @pl.when(pl.program_id(0) == 0)
    def _():
        counts = cnt_ref[...]
        padded = jnp.floor((counts + (bm - 1)) * (1.0 / bm)) * bm
        lane8 = lax.broadcasted_iota(jnp.int32, (8, ROUTER_LANES), 1)
        incl = padded
        shift = 1
        while shift < ROUTER_LANES:
            incl = incl + jnp.where(lane8 >= shift, pltpu.roll(incl, shift, 1), 0.0)
            shift *= 2
        pstart[...] = (incl - padded)[0:1]
        carry[...] = jnp.zeros_like(carry)

    lane = lax.broadcasted_iota(jnp.int32, (t, ROUTER_LANES), 1)
    ri = ri_ref[...]
    oh0 = lane == ri[:, 0:1]
    oh1 = lane == ri[:, 1:2]
    ohf = jnp.where(oh0, 1.0, 0.0) + jnp.where(oh1, 1.0, 0.0)
    prefix = jnp.dot(lower_ref[...], ohf.astype(BF16), preferred_element_type=F32)
    base = prefix + carry[...] + pstart[...]
    d0 = jnp.sum(jnp.where(oh0, base, 0.0), axis=1, keepdims=True)
    d1 = jnp.sum(jnp.where(oh1, base, 0.0), axis=1, keepdims=True)
    dest_ref[...] = jnp.where(lane == 0, d0, jnp.where(lane == 1, d1, 0.0)).astype(jnp.int32)
    carry[...] += jnp.sum(ohf, axis=0, keepdims=True)


def _rank(ri, cnt, bm, t):
    n = ri.shape[0]
    lower = jnp.tril(jnp.ones((t, t), BF16), -1)
    return pl.pallas_call(
        functools.partial(_rank_kernel, bm=bm),
        grid=(n // t,),
        in_specs=[
            pl.BlockSpec((t, ROUTER_LANES), lambda i: (i, 0)),
            pl.BlockSpec((8, ROUTER_LANES), lambda i: (0, 0)),
            pl.BlockSpec((t, t), lambda i: (0, 0)),
        ],
        out_specs=pl.BlockSpec((t, ROUTER_LANES), lambda i: (i, 0)),
        out_shape=jax.ShapeDtypeStruct((n, ROUTER_LANES), jnp.int32),
        scratch_shapes=[pltpu.VMEM((1, ROUTER_LANES), F32), pltpu.VMEM((1, ROUTER_LANES), F32)],
        compiler_params=pltpu.CompilerParams(
            dimension_semantics=("arbitrary",), vmem_limit_bytes=V7X_VMEM_LIMIT),
        name="rank",
    )(ri, cnt, lower)


def _dispatch_kernel(dest_ref, pend_ref, padn_ref, nvalid_ref, b_ref, xs_ref, zeros, sem, zsem, *, bm):
    tm = b_ref.shape[0]
    i = pl.program_id(0)

    @pl.when(i == 0)
    def _():
        zeros[...] = jnp.zeros_like(zeros)
        nblk = xs_ref.shape[0] // bm

        def pad_copy(e, g):
            start = pl.multiple_of(pend_ref[e] - (g + 1) * F32_SUBLANES, F32_SUBLANES)
            return pltpu.make_async_copy(zeros.at[pl.ds(0, F32_SUBLANES)], xs_ref.at[pl.ds(start, F32_SUBLANES)], zsem)

        def tail_copy(blk):
            return pltpu.make_async_copy(zeros, xs_ref.at[pl.ds(pl.multiple_of(blk * bm, bm), bm)], zsem)

        def n_groups(e):
            return (padn_ref[e] + (F32_SUBLANES - 1)) // F32_SUBLANES

        def per_expert(e, c):
            lax.fori_loop(0, n_groups(e), lambda g, c2: (pad_copy(e, g).start(), c2)[1], 0)
            return c
        lax.fori_loop(0, N_EXPERTS, per_expert, 0)
        lax.fori_loop(nvalid_ref[0], nblk, lambda blk, c: (tail_copy(blk).start(), c)[1], 0)

        def per_expert_wait(e, c):
            lax.fori_loop(0, n_groups(e), lambda g, c2: (pad_copy(e, g).wait(), c2)[1], 0)
            return c
        lax.fori_loop(0, N_EXPERTS, per_expert_wait, 0)
        lax.fori_loop(nvalid_ref[0], nblk, lambda blk, c: (tail_copy(blk).wait(), c)[1], 0)

    def row_copy(r, k):
        p = dest_ref[(i * tm + r) * TOP_K + k]
        return pltpu.make_async_copy(b_ref.at[pl.ds(r, 1)], xs_ref.at[pl.ds(p, 1)], sem)

    def issue(r8, c):
        for u in range(DMA_UNROLL):
            for k in range(TOP_K):
                row_copy(r8 * DMA_UNROLL + u, k).start(priority=k)
        return c
    lax.fori_loop(0, tm // DMA_UNROLL, issue, 0)

    def drain(r8, c):
        for u in range(DMA_UNROLL):
            for k in range(TOP_K):
                row_copy(r8 * DMA_UNROLL + u, k).wait()
        return c
    lax.fori_loop(0, tm // DMA_UNROLL, drain, 0)


def _dispatch(dest, pend, padn, nvalid, b_pk, n_slots, tm, bm):
    n, hw = b_pk.shape
    grid_spec = pltpu.PrefetchScalarGridSpec(
        num_scalar_prefetch=4,
        grid=(n // tm,),
        in_specs=[pl.BlockSpec((tm, hw), lambda i, *_: (i, 0))],
        out_specs=pl.BlockSpec(memory_space=pl.ANY),
        scratch_shapes=[pltpu.VMEM((bm, hw), F32), pltpu.SemaphoreType.DMA(()),
                        pltpu.SemaphoreType.DMA(())],
    )
    return pl.pallas_call(
        functools.partial(_dispatch_kernel, bm=bm),
        grid_spec=grid_spec,
        out_shape=jax.ShapeDtypeStruct((n_slots, hw), F32),
        compiler_params=pltpu.CompilerParams(
            dimension_semantics=("arbitrary",), vmem_limit_bytes=V7X_VMEM_LIMIT,
            has_side_effects=True),
        name="dispatch",
    )(dest, pend, padn, nvalid, b_pk)


def _expert_kernel(blk_e_ref, nvalid_ref, eslot_ref, nxt_e_ref, xs_ref, wg_hbm, wu_hbm, wd_hbm, y_ref,
                   wg_f32, wu_f32, wd_f32, wg_bf, wu_bf, wd_bf, sem):
    i = pl.program_id(0)
    nvalid = nvalid_ref[0]
    e = blk_e_ref[i]
    slot = eslot_ref[i]
    new_expert = jnp.logical_or(i == 0, e != blk_e_ref[jnp.maximum(i - 1, 0)])

    def fetch(expert, s):
        return (pltpu.make_async_copy(wg_hbm.at[expert], wg_f32.at[s], sem.at[0, s]),
                pltpu.make_async_copy(wu_hbm.at[expert], wu_f32.at[s], sem.at[1, s]),
                pltpu.make_async_copy(wd_hbm.at[expert], wd_f32.at[s], sem.at[2, s]))

    @pl.when(jnp.logical_and(i == 0, nvalid > 0))
    def _():
        for c in fetch(e, slot):
            c.start()

    @pl.when(jnp.logical_and(new_expert, i < nvalid))
    def _():
        for c in fetch(e, slot):
            c.wait()
        for static_slot in range(2):
            @pl.when(slot == static_slot)
            def _(static_slot=static_slot):
                wg_bf[...] = wg_f32[static_slot].astype(BF16)
                wu_bf[...] = wu_f32[static_slot].astype(BF16)
                wd_bf[...] = wd_f32[static_slot].astype(BF16)

        @pl.when(nxt_e_ref[i] >= 0)
        def _():
            for c in fetch(nxt_e_ref[i], 1 - slot):
                c.start()

    @pl.when(i < nvalid)
    def _():
        xb = xs_ref[...].astype(BF16)
        hg = jnp.dot(xb, wg_bf[...], preferred_element_type=F32)
        hu = jnp.dot(xb, wu_bf[...], preferred_element_type=F32)
        h = (hg * jax.nn.sigmoid(hg) * hu).astype(BF16)
        y_ref[...] = jnp.dot(h, wd_bf[...], preferred_element_type=F32)

    @pl.when(i >= nvalid)
    def _():
        y_ref[...] = jnp.zeros_like(y_ref)


def _experts(blk_e, nvalid, eslot, nxt_e, xs, w_gate, w_up, w_down, bm):
    n_slots, hw = xs.shape
    nblk = n_slots // bm
    _, d, de = w_gate.shape
    last = lambda i, nv: jnp.maximum(jnp.minimum(i, nv[0] - 1), 0)
    grid_spec = pltpu.PrefetchScalarGridSpec(
        num_scalar_prefetch=4,
        grid=(nblk,),
        in_specs=[
            pl.BlockSpec((bm, hw), lambda i, be, nv, es, ne: (last(i, nv), 0)),
            pl.BlockSpec(memory_space=pl.ANY),
            pl.BlockSpec(memory_space=pl.ANY),
            pl.BlockSpec(memory_space=pl.ANY),
        ],
        out_specs=pl.BlockSpec((bm, hw), lambda i, be, nv, es, ne: (i, 0)),
        scratch_shapes=[pltpu.VMEM((2, d, de), F32), pltpu.VMEM((2, d, de), F32),
                        pltpu.VMEM((2, de, d), F32),
                        pltpu.VMEM((d, de), BF16), pltpu.VMEM((d, de), BF16),
                        pltpu.VMEM((de, d), BF16), pltpu.SemaphoreType.DMA((3, 2))],
    )
    return pl.pallas_call(
        _expert_kernel,
        grid_spec=grid_spec,
        out_shape=jax.ShapeDtypeStruct((n_slots, hw), F32),
        compiler_params=pltpu.CompilerParams(
            dimension_semantics=("arbitrary",), vmem_limit_bytes=V7X_VMEM_LIMIT),
        name="experts",
    )(blk_e, nvalid, eslot, nxt_e, xs, w_gate, w_up, w_down)


def _combine_kernel(dest_ref, h1_ref, rg_ref, y_hbm, out_ref, ybuf, sem):
    tm = h1_ref.shape[0]
    i = pl.program_id(0)
    n_steps = pl.num_programs(0)
    slot = i % 2

    def row_copy(tile, r, k, s):
        p = dest_ref[(tile * tm + r) * TOP_K + k]
        return pltpu.make_async_copy(y_hbm.at[pl.ds(p, 1)], ybuf.at[s, k, pl.ds(r, 1)], sem.at[s])

    def issue(tile, s):
        def body(r8, c):
            for u in range(DMA_UNROLL):
                for k in range(TOP_K):
                    row_copy(tile, r8 * DMA_UNROLL + u, k, s).start(priority=k)
            return c
        lax.fori_loop(0, tm // DMA_UNROLL, body, 0)

    @pl.when(i == 0)
    def _():
        issue(0, 0)

    @pl.when(i + 1 < n_steps)
    def _():
        issue(i + 1, 1 - slot)

    def drain(r8, c):
        for u in range(DMA_UNROLL):
            for k in range(TOP_K):
                row_copy(i, r8 * DMA_UNROLL + u, k, slot).wait()
        return c
    lax.fori_loop(0, tm // DMA_UNROLL, drain, 0)

    rg = rg_ref[...]
    out = h1_ref[...]
    for k in range(TOP_K):
        out = out + rg[:, k:k + 1] * ybuf[slot, k]
    out_ref[...] = out


def _combine(dest, h1, rg, y, tm):
    n, d = h1.shape
    hw = y.shape[1]
    row = lambda i, de: (i, 0)
    grid_spec = pltpu.PrefetchScalarGridSpec(
        num_scalar_prefetch=1,
        grid=(n // tm,),
        in_specs=[
            pl.BlockSpec((tm, d), row),
            pl.BlockSpec((tm, ROUTER_LANES), row),
            pl.BlockSpec(memory_space=pl.ANY),
        ],
        out_specs=pl.BlockSpec((tm, d), row),
        scratch_shapes=[pltpu.VMEM((2, TOP_K, tm, hw), F32), pltpu.SemaphoreType.DMA((2,))],
    )
    return pl.pallas_call(
        _combine_kernel,
        grid_spec=grid_spec,
        out_shape=jax.ShapeDtypeStruct((n, d), F32),
        compiler_params=pltpu.CompilerParams(
            dimension_semantics=("arbitrary",), vmem_limit_bytes=V7X_VMEM_LIMIT),
        name="combine",
    )(dest, h1, rg, y)


def _rope_tables(seq, tm):
    rows = seq // GRID_W
    inv_freq = jnp.power(ROPE_THETA, -jnp.arange(0, ROPE_AXIS_DIM, 2, dtype=F32) / ROPE_AXIS_DIM)
    ang_r = jnp.arange(rows).astype(F32)[:, None] * inv_freq[None, :]
    ang_c = jnp.arange(GRID_W).astype(F32)[:, None] * inv_freq[None, :]
    zr = jnp.zeros((rows, ROPE_AXIS_DIM), F32)
    zc = jnp.zeros((GRID_W, ROPE_AXIS_DIM), F32)
    cr, sr, cc, sc = jnp.cos(ang_r), jnp.sin(ang_r), jnp.cos(ang_c), jnp.sin(ang_c)
    per_tile = lambda a: a.reshape(seq // tm, tm // GRID_W, HEAD_DIM)
    crow = per_tile(jnp.concatenate([cr, cr, zr], axis=-1))
    srow = per_tile(jnp.concatenate([-sr, sr, zr], axis=-1))
    ccol = jnp.concatenate([zc, cc, cc], axis=-1)
    scol = jnp.concatenate([zc, -sc, sc], axis=-1)
    return crow, srow, ccol, scol


def _block_plan(counts, nblk, bm):
    padded = (counts + bm - 1) // bm * bm
    pends = jnp.cumsum(padded)
    blk_start = jnp.arange(nblk, dtype=jnp.int32) * bm
    blk_e = jnp.minimum(jnp.sum(pends[None, :] <= blk_start[:, None], axis=1), N_EXPERTS - 1)
    nvalid = (pends[-1] // bm).reshape(1)
    padn = padded - counts
    used = counts > 0
    ids = jnp.arange(N_EXPERTS, dtype=jnp.int32)
    ordinal = jnp.cumsum(used) - 1
    later_used = jnp.logical_and(used[None, :], ids[None, :] > ids[:, None])
    nxt = jnp.min(jnp.where(later_used, ids[None, :], N_EXPERTS), axis=1)
    nxt = jnp.where(nxt == N_EXPERTS, -1, nxt)
    i32 = lambda a: a.astype(jnp.int32)
    return (i32(blk_e), i32(nvalid), i32(ordinal % 2)[blk_e], i32(nxt)[blk_e], i32(pends), i32(padn))


def kernel(x, meta_tokens, norm1_g, w_in, q_norm_g, k_norm_g, pool_w, pool_scale, w_out, norm2_g,
           w_router_group, b_router_group, w_router_expert, b_router_expert, w_gate, w_up, w_down):
    batch, seq, d = x.shape
    n = batch * seq
    assert norm1_g.shape[0] == 1, "single-layer block"
    assert seq % GRID_W == 0 and seq % 256 == 0

    tm_in = _tile(seq // 2, 512)
    tm_post = _tile(seq, 512)
    tq = _tile(seq, 256)
    bm = 256
    tm_moe = _tile(seq, 512)

    x2d = x.reshape(n, d)
    w_in_bf = w_in[0].astype(BF16)
    tables = _rope_tables(seq, tm_in)
    qg = q_norm_g[0].reshape(1, HEAD_DIM)
    kg = k_norm_g[0].reshape(1, HEAD_DIM)
    g1 = norm1_g[0].reshape(1, d)

    qt, k, vt, pp = _inproj(x2d, g1, w_in_bf, tables, qg, kg, batch, seq, tm_in, 2, True)
    _, km, vmt, pm = _inproj(meta_tokens.astype(F32), g1, w_in_bf, tables, qg, kg, 1, N_META, N_META,
                             1, False)

    attn = _attention(qt, k, vt, km[0], vmt[0, :, 0], tq)

    wr = jnp.concatenate([w_router_expert[0], w_router_group[0],
                          jnp.zeros((d, ROUTER_LANES - N_EXPERTS - N_EXPERT_GROUPS), F32)], axis=1)
    br = jnp.concatenate([b_router_expert[0], b_router_group[0],
                          jnp.zeros((ROUTER_LANES - N_EXPERTS - N_EXPERT_GROUPS,), F32)])
    h1, b_pk, ri, rg, cnt = _post(attn.reshape(n, ATTN_WIDTH), pp, pm, x2d, pool_w[0].astype(BF16),
                             pool_scale[0].reshape(1, POOL_WIDTH), w_out[0].astype(BF16),
                             norm2_g[0].reshape(1, d), wr.astype(BF16),
                             br.reshape(1, ROUTER_LANES), seq, tm_post, 2)

    nblk = -(-(n * TOP_K) // bm) + N_EXPERTS
    dest = _rank(ri, cnt, bm, _tile(seq, 1024))[:, :TOP_K].reshape(-1)
    blk_e, nvalid, eslot, nxt_e, pends, padn = _block_plan(cnt[0, :N_EXPERTS].astype(jnp.int32), nblk, bm)
    xs = _dispatch(dest, pends, padn, nvalid, b_pk, nblk * bm, tm_moe, bm)
    y = _experts(blk_e, nvalid, eslot, nxt_e, xs, w_gate[0], w_up[0], w_down[0], bm)
    out = _combine(dest, h1, rg, y, tm_moe)
    return out.reshape(batch, seq, d)
```
